```python
import math
import jax, jax.numpy as jnp
from jax import lax
import numpy as np

D_MODEL = 2048
BATCH = 16
SEQ = 2048
DEPTH = 2

N_A = DEPTH // 2
N_B = DEPTH - N_A
CONV_WIDTH = 31
HEAD_DIM = 64
N_HEADS = D_MODEL // HEAD_DIM
N_KV_HEADS = 8
GROUP = N_HEADS // N_KV_HEADS
WINDOW = 128
BLOCK = 128
D_FF = 4 * D_MODEL
NORM_EPS = 1e-6
LN_EPS = 1e-5

kernel_name = "yoco_conformer_swa_sink_hybrid"


def rms_norm(x, g):
    xf = x.astype(jnp.float32)
    y = xf * lax.rsqrt(jnp.mean(xf * xf, axis=-1, keepdims=True) + NORM_EPS)
    return (y * g.astype(jnp.float32)).astype(x.dtype)


def layer_norm(x, g, b):
    xf = x.astype(jnp.float32)
    mu = jnp.mean(xf, axis=-1, keepdims=True)
    var = jnp.mean(jnp.square(xf - mu), axis=-1, keepdims=True)
    y = (xf - mu) * lax.rsqrt(var + LN_EPS)
    return (y * g.astype(jnp.float32) + b.astype(jnp.float32)).astype(x.dtype)


def conformer_conv(h, w_in, b_in, w_dw, b_dw, ln_g, ln_b, w_out, b_out):
    u = h @ w_in + b_in
    a, gate = jnp.split(u, 2, axis=-1)
    u = a * jax.nn.sigmoid(gate)
    u = lax.conv_general_dilated(
        u, w_dw[:, None, :].astype(u.dtype), window_strides=(1,),
        padding=((CONV_WIDTH - 1, 0),),
        dimension_numbers=("NWC", "WIO", "NWC"),
        feature_group_count=D_MODEL) + b_dw
    u = jax.nn.silu(layer_norm(u, ln_g, ln_b))
    return u @ w_out + b_out


def swa_sink_attention(h, w_q, b_q, sinks, w_o, b_o, k, v):
    B, S, _ = h.shape
    nb = S // BLOCK
    q = (h @ w_q + b_q).reshape(B, nb, BLOCK, N_KV_HEADS, GROUP, HEAD_DIM)

    def band(t):
        tb = t.reshape(B, nb, BLOCK, N_KV_HEADS, HEAD_DIM)
        prev = jnp.concatenate([jnp.zeros_like(tb[:, :1]), tb[:, :-1]], axis=1)
        return jnp.concatenate([prev, tb], axis=2)

    kw = jnp.moveaxis(band(k), 1, 0)
    vw = jnp.moveaxis(band(v), 1, 0)
    qm = jnp.moveaxis(q, 1, 0)
    scale = 1.0 / math.sqrt(HEAD_DIM)
    sink_logit = sinks.astype(jnp.float32).reshape(N_KV_HEADS, GROUP)

    def one_block(args):
        n, qb, kb, vb = args
        s = jnp.einsum("bqkgd,bckd->bkgqc", qb, kb).astype(jnp.float32) * scale
        q_pos = n * BLOCK + jnp.arange(BLOCK)
        k_pos = (n - 1) * BLOCK + jnp.arange(2 * BLOCK)
        diff = q_pos[:, None] - k_pos[None, :]
        mask = (diff >= 0) & (diff < WINDOW) & (k_pos[None, :] >= 0)
        s = jnp.where(mask, s, -jnp.inf)
        sink = jnp.broadcast_to(sink_logit[None, :, :, None, None], s.shape[:-1] + (1,))
        p = jax.nn.softmax(jnp.concatenate([s, sink], axis=-1), axis=-1)[..., :-1]
        return jnp.einsum("bkgqc,bckd->bqkgd", p.astype(vb.dtype), vb)

    o = lax.map(one_block, (jnp.arange(nb), qm, kw, vw))
    o = jnp.moveaxis(o, 0, 1).reshape(B, S, N_HEADS * HEAD_DIM)
    return o @ w_o + b_o


def sqrelu_mlp(h, w_up, w_down):
    return jnp.square(jax.nn.relu(h @ w_up)) @ w_down


def setup_inputs(seed: int = 0) -> dict:
    key = jax.random.key(seed)
    ks = iter(jax.random.split(key, 32))
    f32 = jnp.float32

    def w(shape, fan_in):
        return jax.random.normal(next(ks), shape, f32) * (fan_in ** -0.5)

    def gain(shape):
        return 1.0 + 0.02 * jax.random.normal(next(ks), shape, f32)

    def bias(shape):
        return 0.02 * jax.random.normal(next(ks), shape, f32)

    D, HD_Q, HD_KV = D_MODEL, N_HEADS * HEAD_DIM, N_KV_HEADS * HEAD_DIM
    return {
        "x": jax.random.normal(next(ks), (BATCH, SEQ, D), f32),
        "a_norm": gain((N_A, D)),
        "a_w_in": w((N_A, D, 2 * D), D),
        "a_b_in": bias((N_A, 2 * D)),
        "a_w_dw": w((N_A, CONV_WIDTH, D), CONV_WIDTH),
        "a_b_dw": bias((N_A, D)),
        "a_ln_g": gain((N_A, D)),
        "a_ln_b": bias((N_A, D)),
        "a_w_out": w((N_A, D, D), D),
        "a_b_out": bias((N_A, D)),
        "kv_norm": gain((D,)),
        "w_k": w((D, HD_KV), D),
        "b_k": bias((HD_KV,)),
        "w_v": w((D, HD_KV), D),
        "b_v": bias((HD_KV,)),
        "b_norm": gain((N_B, D)),
        "b_w_q": w((N_B, D, HD_Q), D),
        "b_b_q": bias((N_B, HD_Q)),
        "b_sinks": 0.5 * jax.random.normal(next(ks), (N_B, N_HEADS), f32),
        "b_w_o": w((N_B, HD_Q, D), HD_Q),
        "b_b_o": bias((N_B, D)),
        "mlp_norm": gain((DEPTH, D)),
        "mlp_w_up": w((DEPTH, D, D_FF), D),
        "mlp_w_down": w((DEPTH, D_FF, D), D_FF),
        "final_norm": gain((D,)),
    }


def reference(x, a_norm, a_w_in, a_b_in, a_w_dw, a_b_dw, a_ln_g, a_ln_b, a_w_out, a_b_out,
              kv_norm, w_k, b_k, w_v, b_v,
              b_norm, b_w_q, b_b_q, b_sinks, b_w_o, b_b_o,
              mlp_norm, mlp_w_up, mlp_w_down, final_norm):
    B, S, _ = x.shape
    h = x
    k = v = None
    for i in range(DEPTH):
        if i < N_A:
            h = h + conformer_conv(rms_norm(h, a_norm[i]), a_w_in[i], a_b_in[i], a_w_dw[i],
                                   a_b_dw[i], a_ln_g[i], a_ln_b[i], a_w_out[i], a_b_out[i])
        else:
            j = i - N_A
            if j == 0:
                hk = rms_norm(h, kv_norm)
                k = (hk @ w_k + b_k).reshape(B, S, N_KV_HEADS, HEAD_DIM)
                v = (hk @ w_v + b_v).reshape(B, S, N_KV_HEADS, HEAD_DIM)
            h = h + swa_sink_attention(rms_norm(h, b_norm[j]), b_w_q[j], b_b_q[j], b_sinks[j],
                                       b_w_o[j], b_b_o[j], k, v)
        h = h + sqrelu_mlp(rms_norm(h, mlp_norm[i]), mlp_w_up[i], mlp_w_down[i])
    return rms_norm(h, final_norm)
```

```python
import functools
import math

import jax
import jax.numpy as jnp
from jax import lax
from jax.experimental import pallas as pl
from jax.experimental.pallas import tpu as pltpu

D_MODEL = 2048
CONV_WIDTH = 31
HEAD_DIM = 64
N_HEADS = D_MODEL // HEAD_DIM
N_KV_HEADS = 8
GROUP = N_HEADS // N_KV_HEADS
BLOCK = 128
D_FF = 4 * D_MODEL
D_KV = N_KV_HEADS * HEAD_DIM
NORM_EPS = 1e-6
LN_EPS = 1e-5
HALO = 32
LANES = 128
NCHUNK = D_MODEL // LANES

_MIB = 1024 * 1024
BF16 = jnp.bfloat16
F32 = jnp.float32


def _params(sem, vmem_mib):
    return pltpu.CompilerParams(dimension_semantics=sem, vmem_limit_bytes=vmem_mib * _MIB)


def _rms_scale(x):
    return lax.rsqrt(jnp.mean(x * x, axis=-1, keepdims=True) + NORM_EPS)


def _glu_kernel(x_ref, nrm_ref, wa_ref, wg_ref, ba_ref, bg_ref, o_ref, xn_ref):
    @pl.when(pl.program_id(1) == 0)
    def _():
        x = x_ref[...]
        xn_ref[...] = (x * _rms_scale(x) * nrm_ref[...]).astype(BF16)

    xn = xn_ref[...]
    a = jnp.dot(xn, wa_ref[...], preferred_element_type=F32) + ba_ref[...]
    gate = jnp.dot(xn, wg_ref[...], preferred_element_type=F32) + bg_ref[...]
    o_ref[...] = a * jax.nn.sigmoid(gate)


def _glu(x, nrm, w_in, b_in, *, tm=1024, tn=512):
    t, d = x.shape
    nj = d // tn
    return pl.pallas_call(
        _glu_kernel,
        out_shape=jax.ShapeDtypeStruct((t, d), F32),
        grid=(t // tm, nj),
        in_specs=[
            pl.BlockSpec((tm, d), lambda i, j: (i, 0)),
            pl.BlockSpec((1, d), lambda i, j: (0, 0)),
            pl.BlockSpec((d, tn), lambda i, j: (0, j)),
            pl.BlockSpec((d, tn), lambda i, j: (0, j + nj)),
            pl.BlockSpec((1, tn), lambda i, j: (0, j)),
            pl.BlockSpec((1, tn), lambda i, j: (0, j + nj)),
        ],
        out_specs=pl.BlockSpec((tm, tn), lambda i, j: (i, j)),
        scratch_shapes=[pltpu.VMEM((tm, d), BF16)],
        compiler_params=_params(("parallel", "arbitrary"), 48),
        name="glu_in",
    )(x, nrm, w_in, w_in, b_in, b_in)


def _conv_kernel(g_ref, gp_ref, x_ref, wdw_ref, bdw_ref, lng_ref, lnb_ref, wo_ref, bo_ref,
                 o_ref, ext_ref, u_ref, y_ref, *, ts, steps, sub):
    s = pl.program_id(1)
    prev = gp_ref[0]
    prev = jnp.where(s == 0, jnp.zeros_like(prev), prev)
    for k in range(NCHUNK):
        cs = slice(k * LANES, (k + 1) * LANES)
        ext_ref[pl.ds(k, HALO, stride=NCHUNK), :] = prev[:, cs]
        ext_ref[pl.ds(HALO * NCHUNK + k, ts, stride=NCHUNK), :] = g_ref[0, :, cs]

    base = HALO - (CONV_WIDTH - 1)
    nrow = steps * NCHUNK

    def conv_body(r, carry):
        r0 = pl.multiple_of(r * nrow, nrow)
        acc = jnp.zeros((steps, NCHUNK, LANES), F32)
        for j in range(CONV_WIDTH):
            xs = ext_ref[pl.ds(r0 + (base + j) * NCHUNK, nrow), :].reshape(steps, NCHUNK, LANES)
            acc = acc + xs * wdw_ref[j * NCHUNK:(j + 1) * NCHUNK, :][None]
        u_ref[pl.ds(r0, nrow), :] = acc.reshape(nrow, LANES)
        return carry

    lax.fori_loop(0, ts // steps, conv_body, 0)

    for rb in range(ts // sub):
        u = jnp.concatenate(
            [u_ref[pl.ds(rb * sub * NCHUNK + k, sub, stride=NCHUNK), :] for k in range(NCHUNK)], axis=1)
        u = u + bdw_ref[...]
        mu = jnp.mean(u, axis=-1, keepdims=True)
        uc = u - mu
        var = jnp.mean(uc * uc, axis=-1, keepdims=True)
        y = uc * lax.rsqrt(var + LN_EPS) * lng_ref[...] + lnb_ref[...]
        y_ref[rb * sub:(rb + 1) * sub, :] = (y * jax.nn.sigmoid(y)).astype(BF16)

    o_ref[0] = x_ref[0] + jnp.dot(y_ref[...], wo_ref[...], preferred_element_type=F32) + bo_ref[...]


def _conv_out(g3, x3, w_dw_flat, b_dw, ln_g, ln_b, w_out, b_out, *, ts=512, steps=8, sub=128):
    b, s, d = g3.shape
    hb = ts // HALO
    vec = lambda: pl.BlockSpec((1, d), lambda bi, si: (0, 0))
    return pl.pallas_call(
        functools.partial(_conv_kernel, ts=ts, steps=steps, sub=sub),
        out_shape=jax.ShapeDtypeStruct((b, s, d), F32),
        grid=(b, s // ts),
        in_specs=[
            pl.BlockSpec((1, ts, d), lambda bi, si: (bi, si, 0)),
            pl.BlockSpec((1, HALO, d), lambda bi, si: (bi, jnp.maximum(si * hb - 1, 0), 0)),
            pl.BlockSpec((1, ts, d), lambda bi, si: (bi, si, 0)),
            pl.BlockSpec((CONV_WIDTH * NCHUNK, LANES), lambda bi, si: (0, 0)),
            vec(), vec(), vec(),
            pl.BlockSpec((d, d), lambda bi, si: (0, 0), pipeline_mode=pl.Buffered(1)),
            vec(),
        ],
        out_specs=pl.BlockSpec((1, ts, d), lambda bi, si: (bi, si, 0)),
        scratch_shapes=[pltpu.VMEM(((ts + HALO) * NCHUNK, LANES), F32),
                        pltpu.VMEM((ts * NCHUNK, LANES), F32),
                        pltpu.VMEM((ts, d), BF16)],
        compiler_params=_params(("parallel", "arbitrary"), 56),
        name="conv_out",
    )(g3, g3, x3, w_dw_flat, b_dw, ln_g, ln_b, w_out, b_out)


def _mlp_kernel(h_ref, nrm_ref, wu_ref, wd_ref, fin_ref, o_ref, xn_ref, *, final):
    j = pl.program_id(1)

    @pl.when(j == 0)
    def _():
        h = h_ref[...]
        xn_ref[...] = (h * _rms_scale(h) * nrm_ref[...]).astype(BF16)
        o_ref[...] = h

    t = jnp.dot(xn_ref[...], wu_ref[...], preferred_element_type=F32)
    t = jnp.maximum(t, 0.0)
    t = (t * t).astype(BF16)
    o_ref[...] += jnp.dot(t, wd_ref[...], preferred_element_type=F32)

    if final:
        @pl.when(j == pl.num_programs(1) - 1)
        def _():
            out = o_ref[...]
            o_ref[...] = out * _rms_scale(out) * fin_ref[...]


def _mlp(h, nrm, w_up, w_down, fin, *, final, tm=512, tf=1024):
    t, d = h.shape
    f = w_up.shape[1]
    return pl.pallas_call(
        functools.partial(_mlp_kernel, final=final),
        out_shape=jax.ShapeDtypeStruct((t, d), F32),
        grid=(t // tm, f // tf),
        in_specs=[
            pl.BlockSpec((tm, d), lambda i, j: (i, 0)),
            pl.BlockSpec((1, d), lambda i, j: (0, 0)),
            pl.BlockSpec((d, tf), lambda i, j: (0, j)),
            pl.BlockSpec((tf, d), lambda i, j: (j, 0)),
            pl.BlockSpec((1, d), lambda i, j: (0, 0)),
        ],
        out_specs=pl.BlockSpec((tm, d), lambda i, j: (i, 0)),
        scratch_shapes=[pltpu.VMEM((tm, d), BF16)],
        compiler_params=_params(("parallel", "arbitrary"), 52),
        name="mlp",
    )(h, nrm, w_up, w_down, fin)


def _qkv_kernel(h_ref, nq_ref, nkv_ref, w_ref, b_ref, o_ref, xn_ref, *, nq_blocks):
    j = pl.program_id(1)

    @pl.when(j == 0)
    def _():
        h = h_ref[...]
        hs = h * _rms_scale(h)
        xn_ref[0] = (hs * nq_ref[...]).astype(BF16)
        xn_ref[1] = (hs * nkv_ref[...]).astype(BF16)

    xn = xn_ref[jnp.where(j < nq_blocks, 0, 1)]
    o_ref[...] = (jnp.dot(xn, w_ref[...], preferred_element_type=F32) + b_ref[...]).astype(BF16)


def _qkv(h, nq, nkv, w_qkv, b_qkv, *, tm=512, tn=1024):
    t, d = h.shape
    n = w_qkv.shape[1]
    return pl.pallas_call(
        functools.partial(_qkv_kernel, nq_blocks=d // tn),
        out_shape=jax.ShapeDtypeStruct((t, n), BF16),
        grid=(t // tm, n // tn),
        in_specs=[
            pl.BlockSpec((tm, d), lambda i, j: (i, 0)),
            pl.BlockSpec((1, d), lambda i, j: (0, 0)),
            pl.BlockSpec((1, d), lambda i, j: (0, 0)),
            pl.BlockSpec((d, tn), lambda i, j: (0, j)),
            pl.BlockSpec((1, tn), lambda i, j: (0, j)),
        ],
        out_specs=pl.BlockSpec((tm, tn), lambda i, j: (i, j)),
        scratch_shapes=[pltpu.VMEM((2, tm, d), BF16)],
        compiler_params=_params(("parallel", "arbitrary"), 52),
        name="qkv_proj",
    )(h, nq, nkv, w_qkv, b_qkv)


def _attn_kernel(q_ref, kc_ref, kp_ref, vc_ref, vp_ref, sink_ref, o_ref):
    n = pl.program_id(1)
    scale = 1.0 / math.sqrt(HEAD_DIM)
    rows = GROUP * BLOCK
    qi = lax.broadcasted_iota(jnp.int32, (rows, BLOCK), 0) % BLOCK
    ki = lax.broadcasted_iota(jnp.int32, (rows, BLOCK), 1)
    own = ki <= qi
    has_prev = n > 0
    nt = (((1,), (1,)), ((), ()))
    for kh in range(N_KV_HEADS):
        ks = slice(kh * HEAD_DIM, (kh + 1) * HEAD_DIM)
        q = jnp.concatenate(
            [q_ref[0, :, (kh * GROUP + g) * HEAD_DIM:(kh * GROUP + g + 1) * HEAD_DIM] for g in range(GROUP)],
            axis=0)
        s_own = lax.dot_general(q, kc_ref[0, :, ks], nt, preferred_element_type=F32) * scale
        s_prev = lax.dot_general(q, kp_ref[0, :, ks], nt, preferred_element_type=F32) * scale
        s_prev = jnp.where(has_prev, s_prev, -jnp.inf)
        s = jnp.where(own, s_own, s_prev)
        sink = sink_ref[kh]
        m = jnp.maximum(jnp.max(s, axis=-1, keepdims=True), sink)
        p = jnp.exp(s - m)
        denom = jnp.sum(p, axis=-1, keepdims=True) + jnp.exp(sink - m)
        p = p / denom
        p_own = jnp.where(own, p, 0.0).astype(BF16)
        p_prev = jnp.where(own, 0.0, p).astype(BF16)
        o = (jnp.dot(p_own, vc_ref[0, :, ks], preferred_element_type=F32)
             + jnp.dot(p_prev, vp_ref[0, :, ks], preferred_element_type=F32))
        for g in range(GROUP):
            h0 = (kh * GROUP + g) * HEAD_DIM
            o_ref[0, :, h0:h0 + HEAD_DIM] = o[g * BLOCK:(g + 1) * BLOCK].astype(BF16)


def _attention(qkv3, sink_rows):
    b, s, _ = qkv3.shape
    d = D_MODEL
    kblk = d // D_KV
    prev = lambda n: jnp.maximum(n - 1, 0)
    return pl.pallas_call(
        _attn_kernel,
        out_shape=jax.ShapeDtypeStruct((b, s, d), BF16),
        grid=(b, s // BLOCK),
        in_specs=[
            pl.BlockSpec((1, BLOCK, d), lambda bi, n: (bi, n, 0)),
            pl.BlockSpec((1, BLOCK, D_KV), lambda bi, n: (bi, n, kblk)),
            pl.BlockSpec((1, BLOCK, D_KV), lambda bi, n: (bi, prev(n), kblk)),
            pl.BlockSpec((1, BLOCK, D_KV), lambda bi, n: (bi, n, kblk + 1)),
            pl.BlockSpec((1, BLOCK, D_KV), lambda bi, n: (bi, prev(n), kblk + 1)),
            pl.BlockSpec((N_KV_HEADS, GROUP * BLOCK, 1), lambda bi, n: (0, 0, 0)),
        ],
        out_specs=pl.BlockSpec((1, BLOCK, d), lambda bi, n: (bi, n, 0)),
        compiler_params=_params(("parallel", "arbitrary"), 32),
        name="swa_attn",
    )(qkv3, qkv3, qkv3, qkv3, qkv3, sink_rows)


def _oproj_kernel(o_ref, w_ref, b_ref, h_ref, out_ref):
    out_ref[...] = h_ref[...] + jnp.dot(o_ref[...], w_ref[...], preferred_element_type=F32) + b_ref[...]


def _oproj(o, w_o, b_o, h, *, tm=1024, tn=1024):
    t, d = h.shape
    return pl.pallas_call(
        _oproj_kernel,
        out_shape=jax.ShapeDtypeStruct((t, d), F32),
        grid=(t // tm, d // tn),
        in_specs=[
            pl.BlockSpec((tm, d), lambda i, j: (i, 0)),
            pl.BlockSpec((d, tn), lambda i, j: (0, j)),
            pl.BlockSpec((1, tn), lambda i, j: (0, j)),
            pl.BlockSpec((tm, tn), lambda i, j: (i, j)),
        ],
        out_specs=pl.BlockSpec((tm, tn), lambda i, j: (i, j)),
        compiler_params=_params(("parallel", "arbitrary"), 48),
        name="attn_oproj",
    )(o, w_o, b_o, h)


def kernel(x, a_norm, a_w_in, a_b_in, a_w_dw, a_b_dw, a_ln_g, a_ln_b, a_w_out, a_b_out, kv_norm, w_k, b_k, w_v, b_v, b_norm, b_w_q, b_b_q, b_sinks, b_w_o, b_b_o, mlp_norm, mlp_w_up, mlp_w_down, final_norm):
    bsz, seq, d = x.shape
    t = bsz * seq
    row = lambda v: v.reshape(1, -1)
    ones = jnp.ones((1, d), F32)

    h = x.reshape(t, d)
    g = _glu(h, row(a_norm[0]), a_w_in[0].astype(BF16), row(a_b_in[0]))
    h = _conv_out(g.reshape(bsz, seq, d), x, a_w_dw[0].reshape(CONV_WIDTH * NCHUNK, LANES),
                  row(a_b_dw[0]), row(a_ln_g[0]), row(a_ln_b[0]),
                  a_w_out[0].astype(BF16), row(a_b_out[0])).reshape(t, d)
    h = _mlp(h, row(mlp_norm[0]), mlp_w_up[0].astype(BF16), mlp_w_down[0].astype(BF16), ones, final=False)

    w_qkv = jnp.concatenate([b_w_q[0], w_k, w_v], axis=1).astype(BF16)
    b_qkv = row(jnp.concatenate([b_b_q[0], b_k, b_v]))
    qkv = _qkv(h, row(b_norm[0]), row(kv_norm), w_qkv, b_qkv)
    sink_rows = jnp.broadcast_to(
        b_sinks[0].reshape(N_KV_HEADS, GROUP, 1, 1), (N_KV_HEADS, GROUP, BLOCK, 1)
    ).reshape(N_KV_HEADS, GROUP * BLOCK, 1)
    o = _attention(qkv.reshape(bsz, seq, -1), sink_rows)
    h = _oproj(o.reshape(t, d), b_w_o[0].astype(BF16), row(b_b_o[0]), h)
    h = _mlp(h, row(mlp_norm[1]), mlp_w_up[1].astype(BF16), mlp_w_down[1].astype(BF16), row(final_norm), final=True)
    return h.reshape(bsz, seq, d)
```

```python
import functools
import math

import jax
import jax.numpy as jnp
from jax import lax
from jax.experimental import pallas as pl
from jax.experimental.pallas import tpu as pltpu

D_MODEL = 2048
CONV_WIDTH = 31
HEAD_DIM = 64
N_HEADS = D_MODEL // HEAD_DIM
N_KV_HEADS = 8
GROUP = N_HEADS // N_KV_HEADS
BLOCK = 128
D_FF = 4 * D_MODEL
D_KV = N_KV_HEADS * HEAD_DIM
NORM_EPS = 1e-6
LN_EPS = 1e-5
HALO = 32
LANES = 128
SUBLANES = 8
N_PAIRS = N_KV_HEADS // 2

_MIB = 1024 * 1024
BF16 = jnp.bfloat16
F32 = jnp.float32


def _params(sem, vmem_mib):
    return pltpu.CompilerParams(dimension_semantics=sem, vmem_limit_bytes=vmem_mib * _MIB)


def _rms_scale(x):
    return lax.rsqrt(jnp.mean(x * x, axis=-1, keepdims=True) + NORM_EPS)


def _gluconv_kernel(x_ref, nrm_ref, w_ref, b_ref, wdw_ref, bdw_ref, o_ref, xn_ref, g_ref, u_ref,
                    *, tm, steps, tiles_per_seq):
    i = pl.program_id(0)
    j = pl.program_id(1)
    cpr = SUBLANES
    hist = HALO * cpr

    @pl.when((i == 0) & (j == 0))
    def _():
        g_ref[...] = jnp.zeros_like(g_ref)

    @pl.when(j == 0)
    def _():
        x = x_ref[...]
        xn_ref[...] = (x * _rms_scale(x) * nrm_ref[...]).astype(BF16)

    gb = g_ref.at[j]
    tail = gb[tm * cpr:tm * cpr + hist, :]
    gb[0:hist, :] = jnp.where((i % tiles_per_seq) == 0, jnp.zeros_like(tail), tail)

    for c in range(cpr):
        res = jnp.dot(xn_ref[...], w_ref[c], preferred_element_type=F32) + b_ref[c]
        gb[pl.ds(hist + c, tm, stride=cpr), :] = res[:, :LANES] * jax.nn.sigmoid(res[:, LANES:])

    base = HALO - (CONV_WIDTH - 1)
    for r in range(tm // steps):
        acc = jnp.zeros((steps, cpr, LANES), F32)
        for tap in range(CONV_WIDTH):
            r0 = (r * steps + base + tap) * cpr
            xs = gb[r0:r0 + steps * cpr, :].reshape(steps, cpr, LANES)
            acc = acc + xs * wdw_ref[0, tap * cpr:(tap + 1) * cpr, :][None]
        u_ref[r * steps * cpr:(r + 1) * steps * cpr, :] = acc.reshape(steps * cpr, LANES)
    for k in range(cpr):
        cs = slice(k * LANES, (k + 1) * LANES)
        o_ref[:, cs] = u_ref[pl.ds(k, tm, stride=cpr), :] + bdw_ref[:, cs]


def _gluconv(x, nrm, w_cat, b_cat, w_dw_tm, b_dw, *, seq, tm=512, steps=16):
    t, d = x.shape
    tn = SUBLANES * LANES
    nj = d // tn
    assert seq % tm == 0
    grows = (tm + HALO) * SUBLANES
    return pl.pallas_call(
        functools.partial(_gluconv_kernel, tm=tm, steps=steps, tiles_per_seq=seq // tm),
        out_shape=jax.ShapeDtypeStruct((t, d), F32),
        grid=(t // tm, nj),
        in_specs=[
            pl.BlockSpec((tm, d), lambda i, j: (i, 0)),
            pl.BlockSpec((1, d), lambda i, j: (0, 0)),
            pl.BlockSpec((SUBLANES, d, 2 * LANES), lambda i, j: (j, 0, 0)),
            pl.BlockSpec((SUBLANES, 1, 2 * LANES), lambda i, j: (j, 0, 0)),
            pl.BlockSpec((1, CONV_WIDTH * SUBLANES, LANES), lambda i, j: (j, 0, 0)),
            pl.BlockSpec((1, tn), lambda i, j: (0, j)),
        ],
        out_specs=pl.BlockSpec((tm, tn), lambda i, j: (i, j)),
        scratch_shapes=[pltpu.VMEM((tm, d), BF16),
                        pltpu.VMEM((nj, grows, LANES), F32),
                        pltpu.VMEM((tm * SUBLANES, LANES), F32)],
        compiler_params=_params(("arbitrary", "arbitrary"), 48),
        name="glu_conv",
    )(x, nrm, w_cat, b_cat, w_dw_tm, b_dw)


def _lnout_kernel(u_ref, x_ref, lng_ref, lnb_ref, wo_ref, bo_ref, o_ref, *, sub):
    for sb in range(u_ref.shape[0] // sub):
        rs = slice(sb * sub, (sb + 1) * sub)
        u = u_ref[rs, :]
        mu = jnp.mean(u, axis=-1, keepdims=True)
        uc = u - mu
        var = jnp.mean(uc * uc, axis=-1, keepdims=True)
        y = uc * lax.rsqrt(var + LN_EPS) * lng_ref[...] + lnb_ref[...]
        y = (y * jax.nn.sigmoid(y)).astype(BF16)
        o_ref[rs, :] = x_ref[rs, :] + jnp.dot(y, wo_ref[...], preferred_element_type=F32) + bo_ref[...]


def _lnout(u, x, ln_g, ln_b, w_out, b_out, *, tm=512, sub=256):
    t, d = u.shape
    vec = lambda: pl.BlockSpec((1, d), lambda i: (0, 0))
    return pl.pallas_call(
        functools.partial(_lnout_kernel, sub=sub),
        out_shape=jax.ShapeDtypeStruct((t, d), F32),
        grid=(t // tm,),
        in_specs=[
            pl.BlockSpec((tm, d), lambda i: (i, 0)),
            pl.BlockSpec((tm, d), lambda i: (i, 0)),
            vec(), vec(),
            pl.BlockSpec((d, d), lambda i: (0, 0), pipeline_mode=pl.Buffered(1)),
            vec(),
        ],
        out_specs=pl.BlockSpec((tm, d), lambda i: (i, 0)),
        compiler_params=_params(("parallel",), 52),
        name="ln_out",
    )(u, x, ln_g, ln_b, w_out, b_out)


def _mlp_kernel(h_ref, nrm_ref, wu_ref, wd_ref, fin_ref, o_ref, xn_ref, *, final):
    j = pl.program_id(1)

    @pl.when(j == 0)
    def _():
        h = h_ref[...]
        xn_ref[...] = (h * _rms_scale(h) * nrm_ref[...]).astype(BF16)
        o_ref[...] = h

    t = jnp.dot(xn_ref[...], wu_ref[...], preferred_element_type=F32)
    t = jnp.maximum(t, 0.0)
    t = (t * t).astype(BF16)
    o_ref[...] += jnp.dot(t, wd_ref[...], preferred_element_type=F32)

    if final:
        @pl.when(j == pl.num_programs(1) - 1)
        def _():
            out = o_ref[...]
            o_ref[...] = out * _rms_scale(out) * fin_ref[...]


def _mlp(h, nrm, w_up, w_down, fin, *, final, tm=512, tf=1024):
    t, d = h.shape
    f = w_up.shape[1]
    return pl.pallas_call(
        functools.partial(_mlp_kernel, final=final),
        out_shape=jax.ShapeDtypeStruct((t, d), F32),
        grid=(t // tm, f // tf),
        in_specs=[
            pl.BlockSpec((tm, d), lambda i, j: (i, 0)),
            pl.BlockSpec((1, d), lambda i, j: (0, 0)),
            pl.BlockSpec((d, tf), lambda i, j: (0, j)),
            pl.BlockSpec((tf, d), lambda i, j: (j, 0)),
            pl.BlockSpec((1, d), lambda i, j: (0, 0)),
        ],
        out_specs=pl.BlockSpec((tm, d), lambda i, j: (i, 0)),
        scratch_shapes=[pltpu.VMEM((tm, d), BF16)],
        compiler_params=_params(("parallel", "arbitrary"), 52),
        name="mlp",
    )(h, nrm, w_up, w_down, fin)


def _qkv_kernel(h_ref, nq_ref, nkv_ref, w_ref, b_ref, o_ref, xn_ref, *, nq_blocks):
    j = pl.program_id(1)

    @pl.when(j == 0)
    def _():
        h = h_ref[...]
        hs = h * _rms_scale(h)
        xn_ref[0] = (hs * nq_ref[...]).astype(BF16)
        xn_ref[1] = (hs * nkv_ref[...]).astype(BF16)

    is_q = j < nq_blocks
    xn = xn_ref[jnp.where(is_q, 0, 1)]
    scale = jnp.where(is_q, 1.0 / math.sqrt(HEAD_DIM), 1.0)
    o_ref[...] = ((jnp.dot(xn, w_ref[...], preferred_element_type=F32) + b_ref[...]) * scale).astype(BF16)


def _qkv(h, nq, nkv, w_qkv, b_qkv, *, tm=512, tn=1024):
    t, d = h.shape
    n = w_qkv.shape[1]
    return pl.pallas_call(
        functools.partial(_qkv_kernel, nq_blocks=d // tn),
        out_shape=jax.ShapeDtypeStruct((t, n), BF16),
        grid=(t // tm, n // tn),
        in_specs=[
            pl.BlockSpec((tm, d), lambda i, j: (i, 0)),
            pl.BlockSpec((1, d), lambda i, j: (0, 0)),
            pl.BlockSpec((1, d), lambda i, j: (0, 0)),
            pl.BlockSpec((d, tn), lambda i, j: (0, j)),
            pl.BlockSpec((1, tn), lambda i, j: (0, j)),
        ],
        out_specs=pl.BlockSpec((tm, tn), lambda i, j: (i, j)),
        scratch_shapes=[pltpu.VMEM((2, tm, d), BF16)],
        compiler_params=_params(("parallel", "arbitrary"), 52),
        name="qkv_proj",
    )(h, nq, nkv, w_qkv, b_qkv)


def _attn_kernel(sink_ref, q_ref, kc_ref, kp_ref, vc_ref, vp_ref, o_ref):
    n = pl.program_id(1)
    has_prev = n > 0
    lo = lax.broadcasted_iota(jnp.int32, (BLOCK, LANES), 1) < HEAD_DIM
    qi = lax.broadcasted_iota(jnp.int32, (BLOCK, BLOCK), 0)
    ki = lax.broadcasted_iota(jnp.int32, (BLOCK, BLOCK), 1)
    own = ki <= qi
    m_lo = jnp.where(lo, 1.0, 0.0).astype(BF16)
    m_hi = jnp.where(lo, 0.0, 1.0).astype(BF16)
    nt = (((1,), (1,)), ((), ()))

    for p in range(N_PAIRS):
        cs = slice(p * LANES, (p + 1) * LANES)
        kp, kc, vp, vc = kp_ref[0, :, cs], kc_ref[0, :, cs], vp_ref[0, :, cs], vc_ref[0, :, cs]
        kext = jnp.concatenate([kp * m_lo, kc * m_lo, kp * m_hi, kc * m_hi], axis=0)
        vext = jnp.concatenate([
            jnp.concatenate([vp * m_lo, m_lo], axis=1),
            jnp.concatenate([vc * m_lo, m_lo], axis=1),
            jnp.concatenate([vp * m_hi, m_hi], axis=1),
            jnp.concatenate([vc * m_hi, m_hi], axis=1)], axis=0)
        q = jnp.concatenate(
            [q_ref[0, :, (p * GROUP + m) * LANES:(p * GROUP + m + 1) * LANES] for m in range(GROUP)], axis=0)
        s = lax.dot_general(q, kext, nt, preferred_element_type=F32)

        probs, sink_terms = [], []
        for m in range(GROUP):
            sm = s[m * BLOCK:(m + 1) * BLOCK]
            parts, maxes = [], []
            for hh in range(2):
                s_prev = jnp.where(has_prev, sm[:, (2 * hh) * BLOCK:(2 * hh + 1) * BLOCK], -jnp.inf)
                s_own = sm[:, (2 * hh + 1) * BLOCK:(2 * hh + 2) * BLOCK]
                sel = jnp.where(own, s_own, s_prev)
                mx = jnp.maximum(jnp.max(sel, axis=-1, keepdims=True), sink_ref[(2 * p + hh) * GROUP + m])
                e = jnp.exp(sel - mx)
                parts += [jnp.where(own, 0.0, e), jnp.where(own, e, 0.0)]
                maxes.append(mx)
            probs.append(jnp.concatenate(parts, axis=1).astype(BF16))
            sink_tile = jnp.where(lo, sink_ref[(2 * p) * GROUP + m], sink_ref[(2 * p + 1) * GROUP + m])
            sink_terms.append(jnp.exp(sink_tile - jnp.where(lo, maxes[0], maxes[1])))

        res = jnp.dot(jnp.concatenate(probs, axis=0), vext, preferred_element_type=F32)
        for m in range(GROUP):
            rm = res[m * BLOCK:(m + 1) * BLOCK]
            out = rm[:, :LANES] / (rm[:, LANES:] + sink_terms[m])
            o_ref[0, :, (p * GROUP + m) * LANES:(p * GROUP + m + 1) * LANES] = out.astype(BF16)


def _attention(qkv3, sinks):
    b, s, _ = qkv3.shape
    d = D_MODEL
    kblk = d // D_KV
    prev = lambda n: jnp.maximum(n - 1, 0)
    return pl.pallas_call(
        _attn_kernel,
        out_shape=jax.ShapeDtypeStruct((b, s, d), BF16),
        grid_spec=pltpu.PrefetchScalarGridSpec(
            num_scalar_prefetch=1,
            grid=(b, s // BLOCK),
            in_specs=[
                pl.BlockSpec((1, BLOCK, d), lambda bi, n, sk: (bi, n, 0)),
                pl.BlockSpec((1, BLOCK, D_KV), lambda bi, n, sk: (bi, n, kblk)),
                pl.BlockSpec((1, BLOCK, D_KV), lambda bi, n, sk: (bi, prev(n), kblk)),
                pl.BlockSpec((1, BLOCK, D_KV), lambda bi, n, sk: (bi, n, kblk + 1)),
                pl.BlockSpec((1, BLOCK, D_KV), lambda bi, n, sk: (bi, prev(n), kblk + 1)),
            ],
            out_specs=pl.BlockSpec((1, BLOCK, d), lambda bi, n, sk: (bi, n, 0)),
        ),
        compiler_params=_params(("parallel", "arbitrary"), 32),
        name="swa_attn",
    )(sinks, qkv3, qkv3, qkv3, qkv3, qkv3)


def _oproj_kernel(o_ref, w_ref, b_ref, h_ref, out_ref):
    out_ref[...] = h_ref[...] + jnp.dot(o_ref[...], w_ref[...], preferred_element_type=F32) + b_ref[...]


def _oproj(o, w_o, b_o, h, *, tm=1024, tn=1024):
    t, d = h.shape
    return pl.pallas_call(
        _oproj_kernel,
        out_shape=jax.ShapeDtypeStruct((t, d), F32),
        grid=(t // tm, d // tn),
        in_specs=[
            pl.BlockSpec((tm, d), lambda i, j: (i, 0)),
            pl.BlockSpec((d, tn), lambda i, j: (0, j)),
            pl.BlockSpec((1, tn), lambda i, j: (0, j)),
            pl.BlockSpec((tm, tn), lambda i, j: (i, j)),
        ],
        out_specs=pl.BlockSpec((tm, tn), lambda i, j: (i, j)),
        compiler_params=_params(("parallel", "arbitrary"), 48),
        name="attn_oproj",
    )(o, w_o, b_o, h)


def _pair_heads(w):
    lead = w.shape[:-1]
    w = w.reshape(lead + (N_PAIRS, 2, GROUP, HEAD_DIM))
    return jnp.swapaxes(w, -3, -2).reshape(lead + (N_HEADS * HEAD_DIM,))


def kernel(x, a_norm, a_w_in, a_b_in, a_w_dw, a_b_dw, a_ln_g, a_ln_b, a_w_out, a_b_out, kv_norm, w_k, b_k, w_v, b_v, b_norm, b_w_q, b_b_q, b_sinks, b_w_o, b_b_o, mlp_norm, mlp_w_up, mlp_w_down, final_norm):
    bsz, seq, d = x.shape
    t = bsz * seq
    row = lambda v: v.reshape(1, -1)
    ones = jnp.ones((1, d), F32)
    n_a, n_b = a_norm.shape[0], b_norm.shape[0]
    assert n_b == 1, "shared K/V is projected together with the (single) mixer-B layer's queries"
    tn = SUBLANES * LANES

    h = x.reshape(t, d)
    for i in range(n_a + n_b):
        if i < n_a:
            w_dw_tm = a_w_dw[i].reshape(CONV_WIDTH, d // tn, SUBLANES, LANES).transpose(1, 0, 2, 3)
            w_dw_tm = w_dw_tm.reshape(d // tn, CONV_WIDTH * SUBLANES, LANES)
            w_cat = a_w_in[i].reshape(d, 2, d // LANES, LANES).transpose(2, 0, 1, 3).reshape(d // LANES, d, 2 * LANES)
            b_cat = a_b_in[i].reshape(2, d // LANES, LANES).transpose(1, 0, 2).reshape(d // LANES, 1, 2 * LANES)
            u = _gluconv(h, row(a_norm[i]), w_cat.astype(BF16), b_cat, w_dw_tm, row(a_b_dw[i]), seq=seq)
            h = _lnout(u, h, row(a_ln_g[i]), row(a_ln_b[i]), a_w_out[i].astype(BF16), row(a_b_out[i]))
        else:
            l = i - n_a
            if l == 0:
                w_kv = jnp.concatenate([w_k, w_v], axis=1)
                b_kv = jnp.concatenate([b_k, b_v])
            w_qkv = jnp.concatenate([_pair_heads(b_w_q[l]), w_kv], axis=1).astype(BF16)
            b_qkv = row(jnp.concatenate([_pair_heads(b_b_q[l]), b_kv]))
            qkv = _qkv(h, row(b_norm[l]), row(kv_norm), w_qkv, b_qkv)
            o = _attention(qkv.reshape(bsz, seq, -1), b_sinks[l])
            w_o = _pair_heads(b_w_o[l].T).T.astype(BF16)
            h = _oproj(o.reshape(t, d), w_o, row(b_b_o[l]), h)
        last = i == n_a + n_b - 1
        h = _mlp(h, row(mlp_norm[i]), mlp_w_up[i].astype(BF16), mlp_w_down[i].astype(BF16),
                 row(final_norm) if last else ones, final=last)
    return h.reshape(bsz, seq, d)
```

```python
import functools
import math

import jax
import jax.numpy as jnp
from jax import lax
from jax.experimental import pallas as pl
from jax.experimental.pallas import tpu as pltpu

D_MODEL = 2048
CONV_WIDTH = 31
HEAD_DIM = 64
N_HEADS = D_MODEL // HEAD_DIM
N_KV_HEADS = 8
GROUP = N_HEADS // N_KV_HEADS
BLOCK = 128
D_FF = 4 * D_MODEL
D_KV = N_KV_HEADS * HEAD_DIM
NORM_EPS = 1e-6
LN_EPS = 1e-5
HALO = 32
LANES = 128
SUBLANES = 8
N_PAIRS = N_KV_HEADS // 2

_MIB = 1024 * 1024
BF16 = jnp.bfloat16
F32 = jnp.float32


def _params(sem, vmem_mib):
    return pltpu.CompilerParams(dimension_semantics=sem, vmem_limit_bytes=vmem_mib * _MIB)


def _rms_scale(x):
    return lax.rsqrt(jnp.mean(x * x, axis=-1, keepdims=True) + NORM_EPS)


def _gluconv_kernel(x_ref, nrm_ref, w_ref, b_ref, wdw_ref, bdw_ref, o_ref, xn_ref, g_ref, u_ref,
                    *, tm, steps, tiles_per_seq):
    i = pl.program_id(0)
    j = pl.program_id(1)
    cpr = SUBLANES
    hist = HALO * cpr

    @pl.when((i == 0) & (j == 0))
    def _():
        g_ref[...] = jnp.zeros_like(g_ref)

    @pl.when(j == 0)
    def _():
        x = x_ref[...]
        xn_ref[...] = (x * _rms_scale(x) * nrm_ref[...]).astype(BF16)

    gb = g_ref.at[j]
    tail = gb[tm * cpr:tm * cpr + hist, :]
    gb[0:hist, :] = jnp.where((i % tiles_per_seq) == 0, jnp.zeros_like(tail), tail)

    for c in range(cpr):
        res = jnp.dot(xn_ref[...], w_ref[c], preferred_element_type=F32) + b_ref[c]
        gb[pl.ds(hist + c, tm, stride=cpr), :] = res[:, :LANES] * jax.nn.sigmoid(res[:, LANES:])

    base = HALO - (CONV_WIDTH - 1)
    for r in range(tm // steps):
        acc = jnp.zeros((steps, cpr, LANES), F32)
        for tap in range(CONV_WIDTH):
            r0 = (r * steps + base + tap) * cpr
            xs = gb[r0:r0 + steps * cpr, :].reshape(steps, cpr, LANES)
            acc = acc + xs * wdw_ref[0, tap * cpr:(tap + 1) * cpr, :][None]
        u_ref[r * steps * cpr:(r + 1) * steps * cpr, :] = acc.reshape(steps * cpr, LANES)
    for k in range(cpr):
        cs = slice(k * LANES, (k + 1) * LANES)
        o_ref[:, cs] = u_ref[pl.ds(k, tm, stride=cpr), :] + bdw_ref[:, cs]


def _gluconv(x, nrm, w_cat, b_cat, w_dw_tm, b_dw, *, seq, tm=512, steps=16):
    t, d = x.shape
    tn = SUBLANES * LANES
    nj = d // tn
    assert seq % tm == 0
    grows = (tm + HALO) * SUBLANES
    return pl.pallas_call(
        functools.partial(_gluconv_kernel, tm=tm, steps=steps, tiles_per_seq=seq // tm),
        out_shape=jax.ShapeDtypeStruct((t, d), F32),
        grid=(t // tm, nj),
        in_specs=[
            pl.BlockSpec((tm, d), lambda i, j: (i, 0)),
            pl.BlockSpec((1, d), lambda i, j: (0, 0)),
            pl.BlockSpec((SUBLANES, d, 2 * LANES), lambda i, j: (j, 0, 0)),
            pl.BlockSpec((SUBLANES, 1, 2 * LANES), lambda i, j: (j, 0, 0)),
            pl.BlockSpec((1, CONV_WIDTH * SUBLANES, LANES), lambda i, j: (j, 0, 0)),
            pl.BlockSpec((1, tn), lambda i, j: (0, j)),
        ],
        out_specs=pl.BlockSpec((tm, tn), lambda i, j: (i, j)),
        scratch_shapes=[pltpu.VMEM((tm, d), BF16),
                        pltpu.VMEM((nj, grows, LANES), F32),
                        pltpu.VMEM((tm * SUBLANES, LANES), F32)],
        compiler_params=_params(("arbitrary", "arbitrary"), 48),
        name="glu_conv",
    )(x, nrm, w_cat, b_cat, w_dw_tm, b_dw)


def _lnout_kernel(u_ref, x_ref, lng_ref, lnb_ref, wo_ref, bo_ref, o_ref, *, sub):
    for sb in range(u_ref.shape[0] // sub):
        rs = slice(sb * sub, (sb + 1) * sub)
        u = u_ref[rs, :]
        mu = jnp.mean(u, axis=-1, keepdims=True)
        uc = u - mu
        var = jnp.mean(uc * uc, axis=-1, keepdims=True)
        y = uc * lax.rsqrt(var + LN_EPS) * lng_ref[...] + lnb_ref[...]
        y = (y * jax.nn.sigmoid(y)).astype(BF16)
        o_ref[rs, :] = x_ref[rs, :] + jnp.dot(y, wo_ref[...], preferred_element_type=F32) + bo_ref[...]


def _lnout(u, x, ln_g, ln_b, w_out, b_out, *, tm=512, sub=256):
    t, d = u.shape
    vec = lambda: pl.BlockSpec((1, d), lambda i: (0, 0))
    return pl.pallas_call(
        functools.partial(_lnout_kernel, sub=sub),
        out_shape=jax.ShapeDtypeStruct((t, d), F32),
        grid=(t // tm,),
        in_specs=[
            pl.BlockSpec((tm, d), lambda i: (i, 0)),
            pl.BlockSpec((tm, d), lambda i: (i, 0)),
            vec(), vec(),
            pl.BlockSpec((d, d), lambda i: (0, 0), pipeline_mode=pl.Buffered(1)),
            vec(),
        ],
        out_specs=pl.BlockSpec((tm, d), lambda i: (i, 0)),
        compiler_params=_params(("parallel",), 52),
        name="ln_out",
    )(u, x, ln_g, ln_b, w_out, b_out)


def _mlp_kernel(h_ref, nrm_ref, wu_ref, wd_ref, fin_ref, o_ref, xn_ref, *, final, th):
    j = pl.program_id(1)

    @pl.when(j == 0)
    def _():
        h = h_ref[...]
        xn_ref[...] = (h * _rms_scale(h) * nrm_ref[...]).astype(BF16)
        o_ref[...] = h

    for s in range(wu_ref.shape[1] // th):
        cs = slice(s * th, (s + 1) * th)
        t = jnp.dot(xn_ref[...], wu_ref[:, cs], preferred_element_type=F32)
        t = jnp.maximum(t, 0.0)
        t = (t * t).astype(BF16)
        o_ref[...] += jnp.dot(t, wd_ref[cs, :], preferred_element_type=F32)

    if final:
        @pl.when(j == pl.num_programs(1) - 1)
        def _():
            out = o_ref[...]
            o_ref[...] = out * _rms_scale(out) * fin_ref[...]


def _mlp(h, nrm, w_up, w_down, fin, *, layer, final, tm=512, tf=2048, th=1024):
    t, d = h.shape
    f = w_up.shape[2]
    return pl.pallas_call(
        functools.partial(_mlp_kernel, final=final, th=th),
        out_shape=jax.ShapeDtypeStruct((t, d), F32),
        grid=(t // tm, f // tf),
        in_specs=[
            pl.BlockSpec((tm, d), lambda i, j: (i, 0)),
            pl.BlockSpec((1, d), lambda i, j: (0, 0)),
            pl.BlockSpec((None, d, tf), lambda i, j: (layer, 0, j)),
            pl.BlockSpec((None, tf, d), lambda i, j: (layer, j, 0)),
            pl.BlockSpec((1, d), lambda i, j: (0, 0)),
        ],
        out_specs=pl.BlockSpec((tm, d), lambda i, j: (i, 0)),
        scratch_shapes=[pltpu.VMEM((tm, d), BF16)],
        compiler_params=_params(("parallel", "arbitrary"), 62),
        name="mlp",
    )(h, nrm, w_up, w_down, fin)


def _qkv_kernel(h_ref, nq_ref, nkv_ref, w_ref, b_ref, o_ref, *, sub):
    nq = D_MODEL
    scale = 1.0 / math.sqrt(HEAD_DIM)
    for sb in range(h_ref.shape[0] // sub):
        rs = slice(sb * sub, (sb + 1) * sub)
        h = h_ref[rs, :]
        hs = h * _rms_scale(h)
        xq = (hs * nq_ref[...]).astype(BF16)
        xkv = (hs * nkv_ref[...]).astype(BF16)
        q = jnp.dot(xq, w_ref[:, :nq], preferred_element_type=F32) + b_ref[:, :nq]
        kv = jnp.dot(xkv, w_ref[:, nq:], preferred_element_type=F32) + b_ref[:, nq:]
        o_ref[rs, :nq] = (q * scale).astype(BF16)
        o_ref[rs, nq:] = kv.astype(BF16)


def _qkv(h, nq, nkv, w_qkv, b_qkv, *, tm=512, sub=256):
    t, d = h.shape
    n = w_qkv.shape[1]
    return pl.pallas_call(
        functools.partial(_qkv_kernel, sub=sub),
        out_shape=jax.ShapeDtypeStruct((t, n), BF16),
        grid=(t // tm,),
        in_specs=[
            pl.BlockSpec((tm, d), lambda i: (i, 0)),
            pl.BlockSpec((1, d), lambda i: (0, 0)),
            pl.BlockSpec((1, d), lambda i: (0, 0)),
            pl.BlockSpec((d, n), lambda i: (0, 0), pipeline_mode=pl.Buffered(1)),
            pl.BlockSpec((1, n), lambda i: (0, 0)),
        ],
        out_specs=pl.BlockSpec((tm, n), lambda i: (i, 0)),
        compiler_params=_params(("parallel",), 48),
        name="qkv_proj",
    )(h, nq, nkv, w_qkv, b_qkv)


def _attn_kernel(sink_ref, q_ref, kc_ref, kp_ref, vc_ref, vp_ref, o_ref):
    n = pl.program_id(1)
    has_prev = n > 0
    lo = lax.broadcasted_iota(jnp.int32, (BLOCK, LANES), 1) < HEAD_DIM
    qi = lax.broadcasted_iota(jnp.int32, (BLOCK, BLOCK), 0)
    ki = lax.broadcasted_iota(jnp.int32, (BLOCK, BLOCK), 1)
    own = ki <= qi
    m_lo = jnp.where(lo, 1.0, 0.0).astype(BF16)
    m_hi = jnp.where(lo, 0.0, 1.0).astype(BF16)
    nt = (((1,), (1,)), ((), ()))

    for p in range(N_PAIRS):
        cs = slice(p * LANES, (p + 1) * LANES)
        kp, kc, vp, vc = kp_ref[0, :, cs], kc_ref[0, :, cs], vp_ref[0, :, cs], vc_ref[0, :, cs]
        kext = jnp.concatenate([kp * m_lo, kc * m_lo, kp * m_hi, kc * m_hi], axis=0)
        vext = jnp.concatenate([
            jnp.concatenate([vp * m_lo, m_lo], axis=1),
            jnp.concatenate([vc * m_lo, m_lo], axis=1),
            jnp.concatenate([vp * m_hi, m_hi], axis=1),
            jnp.concatenate([vc * m_hi, m_hi], axis=1)], axis=0)
        q = jnp.concatenate(
            [q_ref[0, :, (p * GROUP + m) * LANES:(p * GROUP + m + 1) * LANES] for m in range(GROUP)], axis=0)
        s = lax.dot_general(q, kext, nt, preferred_element_type=F32)

        probs, sink_terms = [], []
        for m in range(GROUP):
            sm = s[m * BLOCK:(m + 1) * BLOCK]
            parts, maxes = [], []
            for hh in range(2):
                s_prev = jnp.where(has_prev, sm[:, (2 * hh) * BLOCK:(2 * hh + 1) * BLOCK], -jnp.inf)
                s_own = sm[:, (2 * hh + 1) * BLOCK:(2 * hh + 2) * BLOCK]
                sel = jnp.where(own, s_own, s_prev)
                mx = jnp.maximum(jnp.max(sel, axis=-1, keepdims=True), sink_ref[(2 * p + hh) * GROUP + m])
                e = jnp.exp(sel - mx)
                parts += [jnp.where(own, 0.0, e), jnp.where(own, e, 0.0)]
                maxes.append(mx)
            probs.append(jnp.concatenate(parts, axis=1).astype(BF16))
            sink_tile = jnp.where(lo, sink_ref[(2 * p) * GROUP + m], sink_ref[(2 * p + 1) * GROUP + m])
            sink_terms.append(jnp.exp(sink_tile - jnp.where(lo, maxes[0], maxes[1])))

        res = jnp.dot(jnp.concatenate(probs, axis=0), vext, preferred_element_type=F32)
        for m in range(GROUP):
            rm = res[m * BLOCK:(m + 1) * BLOCK]
            out = rm[:, :LANES] / (rm[:, LANES:] + sink_terms[m])
            o_ref[0, :, (p * GROUP + m) * LANES:(p * GROUP + m + 1) * LANES] = out.astype(BF16)


def _attention(qkv3, sinks):
    b, s, _ = qkv3.shape
    d = D_MODEL
    kblk = d // D_KV
    prev = lambda n: jnp.maximum(n - 1, 0)
    return pl.pallas_call(
        _attn_kernel,
        out_shape=jax.ShapeDtypeStruct((b, s, d), BF16),
        grid_spec=pltpu.PrefetchScalarGridSpec(
            num_scalar_prefetch=1,
            grid=(b, s // BLOCK),
            in_specs=[
                pl.BlockSpec((1, BLOCK, d), lambda bi, n, sk: (bi, n, 0)),
                pl.BlockSpec((1, BLOCK, D_KV), lambda bi, n, sk: (bi, n, kblk)),
                pl.BlockSpec((1, BLOCK, D_KV), lambda bi, n, sk: (bi, prev(n), kblk)),
                pl.BlockSpec((1, BLOCK, D_KV), lambda bi, n, sk: (bi, n, kblk + 1)),
                pl.BlockSpec((1, BLOCK, D_KV), lambda bi, n, sk: (bi, prev(n), kblk + 1)),
            ],
            out_specs=pl.BlockSpec((1, BLOCK, d), lambda bi, n, sk: (bi, n, 0)),
        ),
        compiler_params=_params(("parallel", "arbitrary"), 32),
        name="swa_attn",
    )(sinks, qkv3, qkv3, qkv3, qkv3, qkv3)


def _oproj_kernel(o_ref, w_ref, b_ref, h_ref, out_ref):
    out_ref[...] = h_ref[...] + jnp.dot(o_ref[...], w_ref[...], preferred_element_type=F32) + b_ref[...]


def _oproj(o, w_o, b_o, h, *, tm=512):
    t, d = h.shape
    return pl.pallas_call(
        _oproj_kernel,
        out_shape=jax.ShapeDtypeStruct((t, d), F32),
        grid=(t // tm,),
        in_specs=[
            pl.BlockSpec((tm, d), lambda i: (i, 0)),
            pl.BlockSpec((d, d), lambda i: (0, 0), pipeline_mode=pl.Buffered(1)),
            pl.BlockSpec((1, d), lambda i: (0, 0)),
            pl.BlockSpec((tm, d), lambda i: (i, 0)),
        ],
        out_specs=pl.BlockSpec((tm, d), lambda i: (i, 0)),
        compiler_params=_params(("parallel",), 48),
        name="attn_oproj",
    )(o, w_o, b_o, h)


def _pair_heads(w):
    lead = w.shape[:-1]
    w = w.reshape(lead + (N_PAIRS, 2, GROUP, HEAD_DIM))
    return jnp.swapaxes(w, -3, -2).reshape(lead + (N_HEADS * HEAD_DIM,))


def kernel(x, a_norm, a_w_in, a_b_in, a_w_dw, a_b_dw, a_ln_g, a_ln_b, a_w_out, a_b_out, kv_norm, w_k, b_k, w_v, b_v, b_norm, b_w_q, b_b_q, b_sinks, b_w_o, b_b_o, mlp_norm, mlp_w_up, mlp_w_down, final_norm):
    bsz, seq, d = x.shape
    t = bsz * seq
    row = lambda v: v.reshape(1, -1)
    ones = jnp.ones((1, d), F32)
    n_a, n_b = a_norm.shape[0], b_norm.shape[0]
    assert n_b == 1, "shared K/V is projected together with the (single) mixer-B layer's queries"
    tn = SUBLANES * LANES
    w_up, w_down = mlp_w_up.astype(BF16), mlp_w_down.astype(BF16)

    h = x.reshape(t, d)
    for i in range(n_a + n_b):
        if i < n_a:
            w_dw_tm = a_w_dw[i].reshape(CONV_WIDTH, d // tn, SUBLANES, LANES).transpose(1, 0, 2, 3)
            w_dw_tm = w_dw_tm.reshape(d // tn, CONV_WIDTH * SUBLANES, LANES)
            w_cat = a_w_in[i].reshape(d, 2, d // LANES, LANES).transpose(2, 0, 1, 3).reshape(d // LANES, d, 2 * LANES)
            b_cat = a_b_in[i].reshape(2, d // LANES, LANES).transpose(1, 0, 2).reshape(d // LANES, 1, 2 * LANES)
            u = _gluconv(h, row(a_norm[i]), w_cat.astype(BF16), b_cat, w_dw_tm, row(a_b_dw[i]), seq=seq)
            h = _lnout(u, h, row(a_ln_g[i]), row(a_ln_b[i]), a_w_out[i].astype(BF16), row(a_b_out[i]))
        else:
            l = i - n_a
            if l == 0:
                w_kv = jnp.concatenate([w_k, w_v], axis=1)
                b_kv = jnp.concatenate([b_k, b_v])
            w_qkv = jnp.concatenate([_pair_heads(b_w_q[l]), w_kv], axis=1).astype(BF16)
            b_qkv = row(jnp.concatenate([_pair_heads(b_b_q[l]), b_kv]))
            qkv = _qkv(h, row(b_norm[l]), row(kv_norm), w_qkv, b_qkv)
            o = _attention(qkv.reshape(bsz, seq, -1), b_sinks[l])
            w_o = _pair_heads(b_w_o[l].T).T.astype(BF16)
            h = _oproj(o.reshape(t, d), w_o, row(b_b_o[l]), h)
        last = i == n_a + n_b - 1
        h = _mlp(h, row(mlp_norm[i]), w_up, w_down, row(final_norm) if last else ones, layer=i, final=last)
    return h.reshape(bsz, seq, d)
```

```python
import functools
import math

import jax
import jax.numpy as jnp
from jax import lax
from jax.experimental import pallas as pl
from jax.experimental.pallas import tpu as pltpu

D_MODEL = 2048
CONV_WIDTH = 31
HEAD_DIM = 64
N_HEADS = D_MODEL // HEAD_DIM
N_KV_HEADS = 8
GROUP = N_HEADS // N_KV_HEADS
BLOCK = 128
D_FF = 4 * D_MODEL
D_KV = N_KV_HEADS * HEAD_DIM
NORM_EPS = 1e-6
LN_EPS = 1e-5
HALO = 32
LANES = 128
SUBLANES = 8
N_PAIRS = N_KV_HEADS // 2

_MIB = 1024 * 1024
BF16 = jnp.bfloat16
F32 = jnp.float32


def _params(sem, vmem_mib):
    return pltpu.CompilerParams(dimension_semantics=sem, vmem_limit_bytes=vmem_mib * _MIB)


def _rms_scale(x):
    return lax.rsqrt(jnp.mean(x * x, axis=-1, keepdims=True) + NORM_EPS)


def _mixer_kernel(x_ref, nrm_ref, w_ref, b_ref, wdw_ref, bdw_ref, lng_ref, lnb_ref, wo_ref, bo_ref, o_ref,
                  xn_ref, gseg_ref, useg_ref, unat_ref, carry_ref, *, tiles_per_seq, steps):
    i = pl.program_id(0)
    tm = x_ref.shape[0]
    nseg = SUBLANES
    seg = tm // nseg
    span = HALO + seg
    base = HALO - (CONV_WIDTH - 1)

    @pl.when(i == 0)
    def _():
        carry_ref[...] = jnp.zeros_like(carry_ref)

    x = x_ref[...]
    xn_ref[...] = (x * _rms_scale(x) * nrm_ref[...]).astype(BF16)
    is_start = (i % tiles_per_seq) == 0

    for c in range(D_MODEL // LANES):
        cs = slice(c * LANES, (c + 1) * LANES)
        res = jnp.dot(xn_ref[...], w_ref[c], preferred_element_type=F32) + b_ref[c]
        g = res[:, :LANES] * jax.nn.sigmoid(res[:, LANES:])
        prev = carry_ref[c]
        ext = jnp.concatenate([jnp.where(is_start, jnp.zeros_like(prev), prev), g], axis=0)
        carry_ref[c] = g[tm - HALO:, :]
        gb = gseg_ref.at[c]
        for s in range(nseg):
            gb[pl.ds(s, span, stride=nseg), :] = ext[s * seg:s * seg + span, :]
        ub = useg_ref.at[c]
        for r in range(seg // steps):
            acc = jnp.zeros((steps, nseg, LANES), F32)
            for tap in range(CONV_WIDTH):
                r0 = (r * steps + base + tap) * nseg
                xs = gb[r0:r0 + steps * nseg, :].reshape(steps, nseg, LANES)
                acc = acc + xs * wdw_ref[c, tap * nseg:(tap + 1) * nseg, :][None]
            ub[r * steps * nseg:(r + 1) * steps * nseg, :] = acc.reshape(steps * nseg, LANES)
        for s in range(nseg):
            unat_ref[s * seg:(s + 1) * seg, cs] = ub[pl.ds(s, seg, stride=nseg), :] + bdw_ref[:, cs]

    u = unat_ref[...]
    mu = jnp.mean(u, axis=-1, keepdims=True)
    uc = u - mu
    var = jnp.mean(uc * uc, axis=-1, keepdims=True)
    y = uc * lax.rsqrt(var + LN_EPS) * lng_ref[...] + lnb_ref[...]
    y = (y * jax.nn.sigmoid(y)).astype(BF16)
    o_ref[...] = x_ref[...] + jnp.dot(y, wo_ref[...], preferred_element_type=F32) + bo_ref[...]


def _mixer_a(x, nrm, w_cat, b_cat, w_dw_b, b_dw, ln_g, ln_b, w_out, b_out, *, seq, tm=256, steps=8):
    t, d = x.shape
    nch = d // LANES
    seg = tm // SUBLANES
    assert seq % tm == 0 and seg % steps == 0
    vec = lambda: pl.BlockSpec((1, d), lambda i: (0, 0))
    resident = lambda shape: pl.BlockSpec(shape, lambda i: (0,) * len(shape), pipeline_mode=pl.Buffered(1))
    return pl.pallas_call(
        functools.partial(_mixer_kernel, tiles_per_seq=seq // tm, steps=steps),
        out_shape=jax.ShapeDtypeStruct((t, d), F32),
        grid=(t // tm,),
        in_specs=[
            pl.BlockSpec((tm, d), lambda i: (i, 0)),
            vec(),
            resident((nch, d, 2 * LANES)),
            resident((nch, 1, 2 * LANES)),
            resident((nch, CONV_WIDTH * SUBLANES, LANES)),
            vec(), vec(), vec(),
            resident((d, d)),
            vec(),
        ],
        out_specs=pl.BlockSpec((tm, d), lambda i: (i, 0)),
        scratch_shapes=[pltpu.VMEM((tm, d), BF16),
                        pltpu.VMEM((nch, (HALO + seg) * SUBLANES, LANES), F32),
                        pltpu.VMEM((nch, seg * SUBLANES, LANES), F32),
                        pltpu.VMEM((tm, d), F32),
                        pltpu.VMEM((nch, HALO, LANES), F32)],
        compiler_params=_params(("arbitrary",), 56),
        name="mixer_a",
    )(x, nrm, w_cat, b_cat, w_dw_b, b_dw, ln_g, ln_b, w_out, b_out)


def _mlp_kernel(h_ref, nrm_ref, wu_ref, wd_ref, fin_ref, o_ref, xn_ref, *, final, th):
    j = pl.program_id(1)

    @pl.when(j == 0)
    def _():
        h = h_ref[...]
        xn_ref[...] = (h * _rms_scale(h) * nrm_ref[...]).astype(BF16)
        o_ref[...] = h

    for s in range(wu_ref.shape[1] // th):
        cs = slice(s * th, (s + 1) * th)
        t = jnp.dot(xn_ref[...], wu_ref[:, cs], preferred_element_type=F32)
        t = jnp.maximum(t, 0.0)
        t = (t * t).astype(BF16)
        o_ref[...] += jnp.dot(t, wd_ref[cs, :], preferred_element_type=F32)

    if final:
        @pl.when(j == pl.num_programs(1) - 1)
        def _():
            out = o_ref[...]
            o_ref[...] = out * _rms_scale(out) * fin_ref[...]


def _mlp(h, nrm, w_up, w_down, fin, *, layer, final, tm=512, tf=2048, th=1024):
    t, d = h.shape
    f = w_up.shape[2]
    return pl.pallas_call(
        functools.partial(_mlp_kernel, final=final, th=th),
        out_shape=jax.ShapeDtypeStruct((t, d), F32),
        grid=(t // tm, f // tf),
        in_specs=[
            pl.BlockSpec((tm, d), lambda i, j: (i, 0)),
            pl.BlockSpec((1, d), lambda i, j: (0, 0)),
            pl.BlockSpec((None, d, tf), lambda i, j: (layer, 0, j)),
            pl.BlockSpec((None, tf, d), lambda i, j: (layer, j, 0)),
            pl.BlockSpec((1, d), lambda i, j: (0, 0)),
        ],
        out_specs=pl.BlockSpec((tm, d), lambda i, j: (i, 0)),
        scratch_shapes=[pltpu.VMEM((tm, d), BF16)],
        compiler_params=_params(("parallel", "arbitrary"), 62),
        name="mlp",
    )(h, nrm, w_up, w_down, fin)


def _qkv_kernel(h_ref, nq_ref, nkv_ref, w_ref, b_ref, o_ref, *, sub):
    nq = D_MODEL
    scale = 1.0 / math.sqrt(HEAD_DIM)
    for sb in range(h_ref.shape[0] // sub):
        rs = slice(sb * sub, (sb + 1) * sub)
        h = h_ref[rs, :]
        hs = h * _rms_scale(h)
        xq = (hs * nq_ref[...]).astype(BF16)
        xkv = (hs * nkv_ref[...]).astype(BF16)
        q = jnp.dot(xq, w_ref[:, :nq], preferred_element_type=F32) + b_ref[:, :nq]
        kv = jnp.dot(xkv, w_ref[:, nq:], preferred_element_type=F32) + b_ref[:, nq:]
        o_ref[rs, :nq] = (q * scale).astype(BF16)
        o_ref[rs, nq:] = kv.astype(BF16)


def _qkv(h, nq, nkv, w_qkv, b_qkv, *, tm=512, sub=256):
    t, d = h.shape
    n = w_qkv.shape[1]
    return pl.pallas_call(
        functools.partial(_qkv_kernel, sub=sub),
        out_shape=jax.ShapeDtypeStruct((t, n), BF16),
        grid=(t // tm,),
        in_specs=[
            pl.BlockSpec((tm, d), lambda i: (i, 0)),
            pl.BlockSpec((1, d), lambda i: (0, 0)),
            pl.BlockSpec((1, d), lambda i: (0, 0)),
            pl.BlockSpec((d, n), lambda i: (0, 0), pipeline_mode=pl.Buffered(1)),
            pl.BlockSpec((1, n), lambda i: (0, 0)),
        ],
        out_specs=pl.BlockSpec((tm, n), lambda i: (i, 0)),
        compiler_params=_params(("parallel",), 48),
        name="qkv_proj",
    )(h, nq, nkv, w_qkv, b_qkv)


def _attn_kernel(sink_ref, q_ref, kc_ref, kp_ref, vc_ref, vp_ref, o_ref):
    n = pl.program_id(1)
    has_prev = n > 0
    lo = lax.broadcasted_iota(jnp.int32, (BLOCK, LANES), 1) < HEAD_DIM
    qi = lax.broadcasted_iota(jnp.int32, (BLOCK, BLOCK), 0)
    ki = lax.broadcasted_iota(jnp.int32, (BLOCK, BLOCK), 1)
    own = ki <= qi
    m_lo = jnp.where(lo, 1.0, 0.0).astype(BF16)
    m_hi = jnp.where(lo, 0.0, 1.0).astype(BF16)
    nt = (((1,), (1,)), ((), ()))

    for p in range(N_PAIRS):
        cs = slice(p * LANES, (p + 1) * LANES)
        kp, kc, vp, vc = kp_ref[0, :, cs], kc_ref[0, :, cs], vp_ref[0, :, cs], vc_ref[0, :, cs]
        kext = jnp.concatenate([kp * m_lo, kc * m_lo, kp * m_hi, kc * m_hi], axis=0)
        vext = jnp.concatenate([
            jnp.concatenate([vp * m_lo, m_lo], axis=1),
            jnp.concatenate([vc * m_lo, m_lo], axis=1),
            jnp.concatenate([vp * m_hi, m_hi], axis=1),
            jnp.concatenate([vc * m_hi, m_hi], axis=1)], axis=0)
        q = jnp.concatenate(
            [q_ref[0, :, (p * GROUP + m) * LANES:(p * GROUP + m + 1) * LANES] for m in range(GROUP)], axis=0)
        s = lax.dot_general(q, kext, nt, preferred_element_type=F32)

        probs, sink_terms = [], []
        for m in range(GROUP):
            sm = s[m * BLOCK:(m + 1) * BLOCK]
            parts, maxes = [], []
            for hh in range(2):
                s_prev = jnp.where(has_prev, sm[:, (2 * hh) * BLOCK:(2 * hh + 1) * BLOCK], -jnp.inf)
                s_own = sm[:, (2 * hh + 1) * BLOCK:(2 * hh + 2) * BLOCK]
                sel = jnp.where(own, s_own, s_prev)
                mx = jnp.maximum(jnp.max(sel, axis=-1, keepdims=True), sink_ref[(2 * p + hh) * GROUP + m])
                e = jnp.exp(sel - mx)
                parts += [jnp.where(own, 0.0, e), jnp.where(own, e, 0.0)]
                maxes.append(mx)
            probs.append(jnp.concatenate(parts, axis=1).astype(BF16))
            sink_tile = jnp.where(lo, sink_ref[(2 * p) * GROUP + m], sink_ref[(2 * p + 1) * GROUP + m])
            sink_terms.append(jnp.exp(sink_tile - jnp.where(lo, maxes[0], maxes[1])))

        res = jnp.dot(jnp.concatenate(probs, axis=0), vext, preferred_element_type=F32)
        for m in range(GROUP):
            rm = res[m * BLOCK:(m + 1) * BLOCK]
            out = rm[:, :LANES] / (rm[:, LANES:] + sink_terms[m])
            o_ref[0, :, (p * GROUP + m) * LANES:(p * GROUP + m + 1) * LANES] = out.astype(BF16)


def _attention(qkv3, sinks):
    b, s, _ = qkv3.shape
    d = D_MODEL
    kblk = d // D_KV
    prev = lambda n: jnp.maximum(n - 1, 0)
    return pl.pallas_call(
        _attn_kernel,
        out_shape=jax.ShapeDtypeStruct((b, s, d), BF16),
        grid_spec=pltpu.PrefetchScalarGridSpec(
            num_scalar_prefetch=1,
            grid=(b, s // BLOCK),
            in_specs=[
                pl.BlockSpec((1, BLOCK, d), lambda bi, n, sk: (bi, n, 0)),
                pl.BlockSpec((1, BLOCK, D_KV), lambda bi, n, sk: (bi, n, kblk)),
                pl.BlockSpec((1, BLOCK, D_KV), lambda bi, n, sk: (bi, prev(n), kblk)),
                pl.BlockSpec((1, BLOCK, D_KV), lambda bi, n, sk: (bi, n, kblk + 1)),
                pl.BlockSpec((1, BLOCK, D_KV), lambda bi, n, sk: (bi, prev(n), kblk + 1)),
            ],
            out_specs=pl.BlockSpec((1, BLOCK, d), lambda bi, n, sk: (bi, n, 0)),
        ),
        compiler_params=_params(("parallel", "arbitrary"), 32),
        name="swa_attn",
    )(sinks, qkv3, qkv3, qkv3, qkv3, qkv3)


def _oproj_kernel(o_ref, w_ref, b_ref, h_ref, out_ref):
    out_ref[...] = h_ref[...] + jnp.dot(o_ref[...], w_ref[...], preferred_element_type=F32) + b_ref[...]


def _oproj(o, w_o, b_o, h, *, tm=512):
    t, d = h.shape
    return pl.pallas_call(
        _oproj_kernel,
        out_shape=jax.ShapeDtypeStruct((t, d), F32),
        grid=(t // tm,),
        in_specs=[
            pl.BlockSpec((tm, d), lambda i: (i, 0)),
            pl.BlockSpec((d, d), lambda i: (0, 0), pipeline_mode=pl.Buffered(1)),
            pl.BlockSpec((1, d), lambda i: (0, 0)),
            pl.BlockSpec((tm, d), lambda i: (i, 0)),
        ],
        out_specs=pl.BlockSpec((tm, d), lambda i: (i, 0)),
        compiler_params=_params(("parallel",), 48),
        name="attn_oproj",
    )(o, w_o, b_o, h)


def _pair_heads(w):
    lead = w.shape[:-1]
    w = w.reshape(lead + (N_PAIRS, 2, GROUP, HEAD_DIM))
    return jnp.swapaxes(w, -3, -2).reshape(lead + (N_HEADS * HEAD_DIM,))


def kernel(x, a_norm, a_w_in, a_b_in, a_w_dw, a_b_dw, a_ln_g, a_ln_b, a_w_out, a_b_out, kv_norm, w_k, b_k, w_v, b_v, b_norm, b_w_q, b_b_q, b_sinks, b_w_o, b_b_o, mlp_norm, mlp_w_up, mlp_w_down, final_norm):
    bsz, seq, d = x.shape
    t = bsz * seq
    row = lambda v: v.reshape(1, -1)
    ones = jnp.ones((1, d), F32)
    n_a, n_b = a_norm.shape[0], b_norm.shape[0]
    assert n_b == 1, "shared K/V is projected together with the (single) mixer-B layer's queries"
    w_up, w_down = mlp_w_up.astype(BF16), mlp_w_down.astype(BF16)

    h = x.reshape(t, d)
    for i in range(n_a + n_b):
        if i < n_a:
            nch = d // LANES
            w_dw_b = jnp.broadcast_to(a_w_dw[i].reshape(CONV_WIDTH, nch, 1, LANES), (CONV_WIDTH, nch, SUBLANES, LANES))
            w_dw_b = w_dw_b.transpose(1, 0, 2, 3).reshape(nch, CONV_WIDTH * SUBLANES, LANES)
            w_cat = a_w_in[i].astype(BF16).reshape(d, 2, nch, LANES).transpose(2, 0, 1, 3).reshape(nch, d, 2 * LANES)
            b_cat = a_b_in[i].reshape(2, nch, LANES).transpose(1, 0, 2).reshape(nch, 1, 2 * LANES)
            h = _mixer_a(h, row(a_norm[i]), w_cat, b_cat, w_dw_b, row(a_b_dw[i]),
                         row(a_ln_g[i]), row(a_ln_b[i]), a_w_out[i].astype(BF16), row(a_b_out[i]), seq=seq)
        else:
            l = i - n_a
            if l == 0:
                w_kv = jnp.concatenate([w_k, w_v], axis=1)
                b_kv = jnp.concatenate([b_k, b_v])
            w_qkv = jnp.concatenate([_pair_heads(b_w_q[l].astype(BF16)), w_kv.astype(BF16)], axis=1)
            b_qkv = row(jnp.concatenate([_pair_heads(b_b_q[l]), b_kv]))
            qkv = _qkv(h, row(b_norm[l]), row(kv_norm), w_qkv, b_qkv)
            o = _attention(qkv.reshape(bsz, seq, -1), b_sinks[l])
            w_o = _pair_heads(b_w_o[l].astype(BF16).T).T
            h = _oproj(o.reshape(t, d), w_o, row(b_b_o[l]), h)
        last = i == n_a + n_b - 1
        h = _mlp(h, row(mlp_norm[i]), w_up, w_down, row(final_norm) if last else ones, layer=i, final=last)
    return h.reshape(bsz, seq, d)
```

```python
import functools
import math

import jax
import jax.numpy as jnp
from jax import lax
from jax.experimental import pallas as pl
from jax.experimental.pallas import tpu as pltpu

D_MODEL = 2048
CONV_WIDTH = 31
HEAD_DIM = 64
N_HEADS = D_MODEL // HEAD_DIM
N_KV_HEADS = 8
GROUP = N_HEADS // N_KV_HEADS
BLOCK = 128
D_FF = 4 * D_MODEL
D_KV = N_KV_HEADS * HEAD_DIM
NORM_EPS = 1e-6
LN_EPS = 1e-5
HALO = 32
LANES = 128
SUBLANES = 8
N_PAIRS = N_KV_HEADS // 2

_MIB = 1024 * 1024
BF16 = jnp.bfloat16
F32 = jnp.float32

V7X_VMEM_MIB = 64
VMEM_MIB = {"mixer_a": 56, "mlp": 62, "qkv_proj": 48, "swa_attn": 32, "attn_oproj": 48}
assert max(VMEM_MIB.values()) < V7X_VMEM_MIB


def _params(sem, name):
    return pltpu.CompilerParams(dimension_semantics=sem, vmem_limit_bytes=VMEM_MIB[name] * _MIB)


def _rms_scale(x):
    return lax.rsqrt(jnp.mean(x * x, axis=-1, keepdims=True) + NORM_EPS)


def _mixer_kernel(x_ref, nrm_ref, w_ref, b_ref, wdw_ref, bdw_ref, lng_ref, lnb_ref, wo_ref, bo_ref, o_ref,
                  xn_ref, gseg_ref, useg_ref, unat_ref, carry_ref, *, tiles_per_seq, steps):
    i = pl.program_id(0)
    tm = x_ref.shape[0]
    nseg = SUBLANES
    seg = tm // nseg
    span = HALO + seg
    base = HALO - (CONV_WIDTH - 1)

    @pl.when(i == 0)
    def _():
        carry_ref[...] = jnp.zeros_like(carry_ref)

    x = x_ref[...]
    xn_ref[...] = (x * _rms_scale(x) * nrm_ref[...]).astype(BF16)
    is_start = (i % tiles_per_seq) == 0

    for c in range(D_MODEL // LANES):
        cs = slice(c * LANES, (c + 1) * LANES)
        res = jnp.dot(xn_ref[...], w_ref[c], preferred_element_type=F32) + b_ref[c]
        g = res[:, :LANES] * jax.nn.sigmoid(res[:, LANES:])
        prev = carry_ref[c]
        ext = jnp.concatenate([jnp.where(is_start, jnp.zeros_like(prev), prev), g], axis=0)
        carry_ref[c] = g[tm - HALO:, :]
        gb = gseg_ref.at[c]
        for s in range(nseg):
            gb[pl.ds(s, span, stride=nseg), :] = ext[s * seg:s * seg + span, :]
        ub = useg_ref.at[c]
        for r in range(seg // steps):
            acc = jnp.zeros((steps, nseg, LANES), F32)
            for tap in range(CONV_WIDTH):
                r0 = (r * steps + base + tap) * nseg
                xs = gb[r0:r0 + steps * nseg, :].reshape(steps, nseg, LANES)
                acc = acc + xs * wdw_ref[c, tap * nseg:(tap + 1) * nseg, :][None]
            ub[r * steps * nseg:(r + 1) * steps * nseg, :] = acc.reshape(steps * nseg, LANES)
        for s in range(nseg):
            unat_ref[s * seg:(s + 1) * seg, cs] = ub[pl.ds(s, seg, stride=nseg), :] + bdw_ref[:, cs]

    u = unat_ref[...]
    mu = jnp.mean(u, axis=-1, keepdims=True)
    uc = u - mu
    var = jnp.mean(uc * uc, axis=-1, keepdims=True)
    y = uc * lax.rsqrt(var + LN_EPS) * lng_ref[...] + lnb_ref[...]
    y = (y * jax.nn.sigmoid(y)).astype(BF16)
    o_ref[...] = x_ref[...] + jnp.dot(y, wo_ref[...], preferred_element_type=F32) + bo_ref[...]


def _mixer_a(x, nrm, w_cat, b_cat, w_dw_b, b_dw, ln_g, ln_b, w_out, b_out, *, seq, tm=256, steps=8):
    t, d = x.shape
    nch = d // LANES
    seg = tm // SUBLANES
    assert seq % tm == 0 and seg % steps == 0
    vec = lambda: pl.BlockSpec((1, d), lambda i: (0, 0))
    resident = lambda shape: pl.BlockSpec(shape, lambda i: (0,) * len(shape), pipeline_mode=pl.Buffered(1))
    return pl.pallas_call(
        functools.partial(_mixer_kernel, tiles_per_seq=seq // tm, steps=steps),
        out_shape=jax.ShapeDtypeStruct((t, d), F32),
        grid=(t // tm,),
        in_specs=[
            pl.BlockSpec((tm, d), lambda i: (i, 0)),
            vec(),
            resident((nch, d, 2 * LANES)),
            resident((nch, 1, 2 * LANES)),
            resident((nch, CONV_WIDTH * SUBLANES, LANES)),
            vec(), vec(), vec(),
            resident((d, d)),
            vec(),
        ],
        out_specs=pl.BlockSpec((tm, d), lambda i: (i, 0)),
        scratch_shapes=[pltpu.VMEM((tm, d), BF16),
                        pltpu.VMEM((nch, (HALO + seg) * SUBLANES, LANES), F32),
                        pltpu.VMEM((nch, seg * SUBLANES, LANES), F32),
                        pltpu.VMEM((tm, d), F32),
                        pltpu.VMEM((nch, HALO, LANES), F32)],
        compiler_params=_params(("arbitrary",), "mixer_a"),
        name="mixer_a",
    )(x, nrm, w_cat, b_cat, w_dw_b, b_dw, ln_g, ln_b, w_out, b_out)


def _mlp_kernel(h_ref, nrm_ref, wu_ref, wd_ref, fin_ref, o_ref, xn_ref, *, final, th):
    j = pl.program_id(1)

    def slices(first):
        for s in range(wu_ref.shape[1] // th):
            cs = slice(s * th, (s + 1) * th)
            t = jnp.dot(xn_ref[...], wu_ref[:, cs], preferred_element_type=F32)
            t = jnp.maximum(t, 0.0)
            t = (t * t).astype(BF16)
            part = jnp.dot(t, wd_ref[cs, :], preferred_element_type=F32)
            if first and s == 0:
                o_ref[...] = h_ref[...] + part
            else:
                o_ref[...] += part

    @pl.when(j == 0)
    def _():
        h = h_ref[...]
        xn_ref[...] = (h * _rms_scale(h) * nrm_ref[...]).astype(BF16)
        slices(True)

    @pl.when(j > 0)
    def _():
        slices(False)

    if final:
        @pl.when(j == pl.num_programs(1) - 1)
        def _():
            out = o_ref[...]
            o_ref[...] = out * _rms_scale(out) * fin_ref[...]


def _mlp(h, nrm, w_up, w_down, fin, *, layer, final, tm=512, tf=2048, th=1024):
    t, d = h.shape
    f = w_up.shape[2]
    return pl.pallas_call(
        functools.partial(_mlp_kernel, final=final, th=th),
        out_shape=jax.ShapeDtypeStruct((t, d), F32),
        grid=(t // tm, f // tf),
        in_specs=[
            pl.BlockSpec((tm, d), lambda i, j: (i, 0)),
            pl.BlockSpec((1, d), lambda i, j: (0, 0)),
            pl.BlockSpec((None, d, tf), lambda i, j: (layer, 0, j)),
            pl.BlockSpec((None, tf, d), lambda i, j: (layer, j, 0)),
            pl.BlockSpec((1, d), lambda i, j: (0, 0)),
        ],
        out_specs=pl.BlockSpec((tm, d), lambda i, j: (i, 0)),
        scratch_shapes=[pltpu.VMEM((tm, d), BF16)],
        compiler_params=_params(("parallel", "arbitrary"), "mlp"),
        name="mlp",
    )(h, nrm, w_up, w_down, fin)


def _qkv_kernel(h_ref, nq_ref, nkv_ref, w_ref, b_ref, o_ref, *, sub):
    nq = D_MODEL
    scale = 1.0 / math.sqrt(HEAD_DIM)
    for sb in range(h_ref.shape[0] // sub):
        rs = slice(sb * sub, (sb + 1) * sub)
        h = h_ref[rs, :]
        hs = h * _rms_scale(h)
        xq = (hs * nq_ref[...]).astype(BF16)
        xkv = (hs * nkv_ref[...]).astype(BF16)
        q = jnp.dot(xq, w_ref[:, :nq], preferred_element_type=F32) + b_ref[:, :nq]
        kv = jnp.dot(xkv, w_ref[:, nq:], preferred_element_type=F32) + b_ref[:, nq:]
        o_ref[rs, :nq] = (q * scale).astype(BF16)
        o_ref[rs, nq:] = kv.astype(BF16)


def _qkv(h, nq, nkv, w_qkv, b_qkv, *, tm=512, sub=256):
    t, d = h.shape
    n = w_qkv.shape[1]
    return pl.pallas_call(
        functools.partial(_qkv_kernel, sub=sub),
        out_shape=jax.ShapeDtypeStruct((t, n), BF16),
        grid=(t // tm,),
        in_specs=[
            pl.BlockSpec((tm, d), lambda i: (i, 0)),
            pl.BlockSpec((1, d), lambda i: (0, 0)),
            pl.BlockSpec((1, d), lambda i: (0, 0)),
            pl.BlockSpec((d, n), lambda i: (0, 0), pipeline_mode=pl.Buffered(1)),
            pl.BlockSpec((1, n), lambda i: (0, 0)),
        ],
        out_specs=pl.BlockSpec((tm, n), lambda i: (i, 0)),
        compiler_params=_params(("parallel",), "qkv_proj"),
        name="qkv_proj",
    )(h, nq, nkv, w_qkv, b_qkv)


def _attn_kernel(sink_ref, q_ref, kc_ref, kp_ref, vc_ref, vp_ref, o_ref):
    n = pl.program_id(1)
    has_prev = n > 0
    lo = lax.broadcasted_iota(jnp.int32, (BLOCK, LANES), 1) < HEAD_DIM
    qi = lax.broadcasted_iota(jnp.int32, (BLOCK, BLOCK), 0)
    ki = lax.broadcasted_iota(jnp.int32, (BLOCK, BLOCK), 1)
    own = ki <= qi
    m_lo = jnp.where(lo, 1.0, 0.0).astype(BF16)
    m_hi = jnp.where(lo, 0.0, 1.0).astype(BF16)
    nt = (((1,), (1,)), ((), ()))

    for p in range(N_PAIRS):
        cs = slice(p * LANES, (p + 1) * LANES)
        kp, kc, vp, vc = kp_ref[0, :, cs], kc_ref[0, :, cs], vp_ref[0, :, cs], vc_ref[0, :, cs]
        kext = jnp.concatenate([kp * m_lo, kc * m_lo, kp * m_hi, kc * m_hi], axis=0)
        vext = jnp.concatenate([
            jnp.concatenate([vp * m_lo, m_lo], axis=1),
            jnp.concatenate([vc * m_lo, m_lo], axis=1),
            jnp.concatenate([vp * m_hi, m_hi], axis=1),
            jnp.concatenate([vc * m_hi, m_hi], axis=1)], axis=0)
        q = jnp.concatenate(
            [q_ref[0, :, (p * GROUP + m) * LANES:(p * GROUP + m + 1) * LANES] for m in range(GROUP)], axis=0)
        s = lax.dot_general(q, kext, nt, preferred_element_type=F32)

        probs, sink_terms = [], []
        for m in range(GROUP):
            sm = s[m * BLOCK:(m + 1) * BLOCK]
            parts, maxes = [], []
            for hh in range(2):
                s_prev = jnp.where(has_prev, sm[:, (2 * hh) * BLOCK:(2 * hh + 1) * BLOCK], -jnp.inf)
                s_own = sm[:, (2 * hh + 1) * BLOCK:(2 * hh + 2) * BLOCK]
                sel = jnp.where(own, s_own, s_prev)
                mx = jnp.maximum(jnp.max(sel, axis=-1, keepdims=True), sink_ref[(2 * p + hh) * GROUP + m])
                e = jnp.exp(sel - mx)
                parts += [jnp.where(own, 0.0, e), jnp.where(own, e, 0.0)]
                maxes.append(mx)
            probs.append(jnp.concatenate(parts, axis=1).astype(BF16))
            sink_tile = jnp.where(lo, sink_ref[(2 * p) * GROUP + m], sink_ref[(2 * p + 1) * GROUP + m])
            sink_terms.append(jnp.exp(sink_tile - jnp.where(lo, maxes[0], maxes[1])))

        res = jnp.dot(jnp.concatenate(probs, axis=0), vext, preferred_element_type=F32)
        for m in range(GROUP):
            rm = res[m * BLOCK:(m + 1) * BLOCK]
            out = rm[:, :LANES] / (rm[:, LANES:] + sink_terms[m])
            o_ref[0, :, (p * GROUP + m) * LANES:(p * GROUP + m + 1) * LANES] = out.astype(BF16)


def _attention(qkv3, sinks):
    b, s, _ = qkv3.shape
    d = D_MODEL
    kblk = d // D_KV
    prev = lambda n: jnp.maximum(n - 1, 0)
    return pl.pallas_call(
        _attn_kernel,
        out_shape=jax.ShapeDtypeStruct((b, s, d), BF16),
        grid_spec=pltpu.PrefetchScalarGridSpec(
            num_scalar_prefetch=1,
            grid=(b, s // BLOCK),
            in_specs=[
                pl.BlockSpec((1, BLOCK, d), lambda bi, n, sk: (bi, n, 0)),
                pl.BlockSpec((1, BLOCK, D_KV), lambda bi, n, sk: (bi, n, kblk)),
                pl.BlockSpec((1, BLOCK, D_KV), lambda bi, n, sk: (bi, prev(n), kblk)),
                pl.BlockSpec((1, BLOCK, D_KV), lambda bi, n, sk: (bi, n, kblk + 1)),
                pl.BlockSpec((1, BLOCK, D_KV), lambda bi, n, sk: (bi, prev(n), kblk + 1)),
            ],
            out_specs=pl.BlockSpec((1, BLOCK, d), lambda bi, n, sk: (bi, n, 0)),
        ),
        compiler_params=_params(("parallel", "arbitrary"), "swa_attn"),
        name="swa_attn",
    )(sinks, qkv3, qkv3, qkv3, qkv3, qkv3)


def _oproj_kernel(o_ref, w_ref, b_ref, h_ref, out_ref):
    out_ref[...] = h_ref[...] + jnp.dot(o_ref[...], w_ref[...], preferred_element_type=F32) + b_ref[...]


def _oproj(o, w_o, b_o, h, *, tm=512):
    t, d = h.shape
    return pl.pallas_call(
        _oproj_kernel,
        out_shape=jax.ShapeDtypeStruct((t, d), F32),
        grid=(t // tm,),
        in_specs=[
            pl.BlockSpec((tm, d), lambda i: (i, 0)),
            pl.BlockSpec((d, d), lambda i: (0, 0), pipeline_mode=pl.Buffered(1)),
            pl.BlockSpec((1, d), lambda i: (0, 0)),
            pl.BlockSpec((tm, d), lambda i: (i, 0)),
        ],
        out_specs=pl.BlockSpec((tm, d), lambda i: (i, 0)),
        compiler_params=_params(("parallel",), "attn_oproj"),
        name="attn_oproj",
    )(o, w_o, b_o, h)


def _pair_heads(w):
    lead = w.shape[:-1]
    w = w.reshape(lead + (N_PAIRS, 2, GROUP, HEAD_DIM))
    return jnp.swapaxes(w, -3, -2).reshape(lead + (N_HEADS * HEAD_DIM,))


def kernel(x, a_norm, a_w_in, a_b_in, a_w_dw, a_b_dw, a_ln_g, a_ln_b, a_w_out, a_b_out, kv_norm, w_k, b_k, w_v, b_v, b_norm, b_w_q, b_b_q, b_sinks, b_w_o, b_b_o, mlp_norm, mlp_w_up, mlp_w_down, final_norm):
    bsz, seq, d = x.shape
    t = bsz * seq
    row = lambda v: v.reshape(1, -1)
    ones = jnp.ones((1, d), F32)
    n_a, n_b = a_norm.shape[0], b_norm.shape[0]
    assert n_b == 1, "shared K/V is projected together with the (single) mixer-B layer's queries"
    w_up, w_down = mlp_w_up.astype(BF16), mlp_w_down.astype(BF16)

    h = x.reshape(t, d)
    for i in range(n_a + n_b):
        if i < n_a:
            nch = d // LANES
            w_dw_b = jnp.broadcast_to(a_w_dw[i].reshape(CONV_WIDTH, nch, 1, LANES), (CONV_WIDTH, nch, SUBLANES, LANES))
            w_dw_b = w_dw_b.transpose(1, 0, 2, 3).reshape(nch, CONV_WIDTH * SUBLANES, LANES)
            w_cat = a_w_in[i].astype(BF16).reshape(d, 2, nch, LANES).transpose(2, 0, 1, 3).reshape(nch, d, 2 * LANES)
            b_cat = a_b_in[i].reshape(2, nch, LANES).transpose(1, 0, 2).reshape(nch, 1, 2 * LANES)
            h = _mixer_a(h, row(a_norm[i]), w_cat, b_cat, w_dw_b, row(a_b_dw[i]),
                         row(a_ln_g[i]), row(a_ln_b[i]), a_w_out[i].astype(BF16), row(a_b_out[i]), seq=seq)
        else:
            l = i - n_a
            if l == 0:
                w_kv = jnp.concatenate([w_k, w_v], axis=1)
                b_kv = jnp.concatenate([b_k, b_v])
            w_qkv = jnp.concatenate([_pair_heads(b_w_q[l].astype(BF16)), w_kv.astype(BF16)], axis=1)
            b_qkv = row(jnp.concatenate([_pair_heads(b_b_q[l]), b_kv]))
            qkv = _qkv(h, row(b_norm[l]), row(kv_norm), w_qkv, b_qkv)
            o = _attention(qkv.reshape(bsz, seq, -1), b_sinks[l])
            w_o = _pair_heads(b_w_o[l].astype(BF16).T).T
            h = _oproj(o.reshape(t, d), w_o, row(b_b_o[l]), h)
        last = i == n_a + n_b - 1
        h = _mlp(h, row(mlp_norm[i]), w_up, w_down, row(final_norm) if last else ones, layer=i, final=last)
    return h.reshape(bsz, seq, d)
```

```python
import functools
import math

import jax
import jax.numpy as jnp
from jax import lax
from jax.experimental import pallas as pl
from jax.experimental.pallas import tpu as pltpu

D_MODEL = 2048
CONV_WIDTH = 31
HEAD_DIM = 64
N_HEADS = D_MODEL // HEAD_DIM
N_KV_HEADS = 8
GROUP = N_HEADS // N_KV_HEADS
BLOCK = 128
D_FF = 4 * D_MODEL
D_KV = N_KV_HEADS * HEAD_DIM
NORM_EPS = 1e-6
LN_EPS = 1e-5
HALO = 32
LANES = 128
SUBLANES = 8
N_PAIRS = N_KV_HEADS // 2

_MIB = 1024 * 1024
BF16 = jnp.bfloat16
F32 = jnp.float32

V7X_VMEM_MIB = 64
VMEM_MIB = {"mixer_a": 56, "mlp": 62, "qkv_proj": 48, "swa_attn": 48}
assert max(VMEM_MIB.values()) < V7X_VMEM_MIB


def _params(sem, name):
    return pltpu.CompilerParams(dimension_semantics=sem, vmem_limit_bytes=VMEM_MIB[name] * _MIB)


def _rms_scale(x):
    return lax.rsqrt(jnp.mean(x * x, axis=-1, keepdims=True) + NORM_EPS)


def _mixer_kernel(x_ref, nrm_ref, w_ref, b_ref, wdw_ref, bdw_ref, lng_ref, lnb_ref, wo_ref, bo_ref, o_ref,
                  xn_ref, gseg_ref, useg_ref, unat_ref, carry_ref, *, tiles_per_seq, steps):
    i = pl.program_id(0)
    tm = x_ref.shape[0]
    nseg = SUBLANES
    seg = tm // nseg
    span = HALO + seg
    base = HALO - (CONV_WIDTH - 1)

    @pl.when(i == 0)
    def _():
        carry_ref[...] = jnp.zeros_like(carry_ref)

    x = x_ref[...]
    xn_ref[...] = (x * _rms_scale(x) * nrm_ref[...]).astype(BF16)
    is_start = (i % tiles_per_seq) == 0

    for c in range(D_MODEL // LANES):
        cs = slice(c * LANES, (c + 1) * LANES)
        res = jnp.dot(xn_ref[...], w_ref[c], preferred_element_type=F32) + b_ref[c]
        g = res[:, :LANES] * jax.nn.sigmoid(res[:, LANES:])
        prev = carry_ref[c]
        ext = jnp.concatenate([jnp.where(is_start, jnp.zeros_like(prev), prev), g], axis=0)
        carry_ref[c] = g[tm - HALO:, :]
        gb = gseg_ref.at[c]
        for s in range(nseg):
            gb[pl.ds(s, span, stride=nseg), :] = ext[s * seg:s * seg + span, :]
        ub = useg_ref.at[c]
        for r in range(seg // steps):
            acc = jnp.zeros((steps, nseg, LANES), F32)
            for tap in range(CONV_WIDTH):
                r0 = (r * steps + base + tap) * nseg
                xs = gb[r0:r0 + steps * nseg, :].reshape(steps, nseg, LANES)
                acc = acc + xs * wdw_ref[c, tap * nseg:(tap + 1) * nseg, :][None]
            ub[r * steps * nseg:(r + 1) * steps * nseg, :] = acc.reshape(steps * nseg, LANES)
        for s in range(nseg):
            unat_ref[s * seg:(s + 1) * seg, cs] = ub[pl.ds(s, seg, stride=nseg), :] + bdw_ref[:, cs]

    u = unat_ref[...]
    mu = jnp.mean(u, axis=-1, keepdims=True)
    uc = u - mu
    var = jnp.mean(uc * uc, axis=-1, keepdims=True)
    y = uc * lax.rsqrt(var + LN_EPS) * lng_ref[...] + lnb_ref[...]
    y = (y * jax.nn.sigmoid(y)).astype(BF16)
    o_ref[...] = x_ref[...] + jnp.dot(y, wo_ref[...], preferred_element_type=F32) + bo_ref[...]


def _mixer_a(x, nrm, w_cat, b_cat, w_dw_b, b_dw, ln_g, ln_b, w_out, b_out, *, seq, tm=256, steps=8):
    t, d = x.shape
    nch = d // LANES
    seg = tm // SUBLANES
    assert seq % tm == 0 and seg % steps == 0
    vec = lambda: pl.BlockSpec((1, d), lambda i: (0, 0))
    resident = lambda shape: pl.BlockSpec(shape, lambda i: (0,) * len(shape), pipeline_mode=pl.Buffered(1))
    return pl.pallas_call(
        functools.partial(_mixer_kernel, tiles_per_seq=seq // tm, steps=steps),
        out_shape=jax.ShapeDtypeStruct((t, d), F32),
        grid=(t // tm,),
        in_specs=[
            pl.BlockSpec((tm, d), lambda i: (i, 0)),
            vec(),
            resident((nch, d, 2 * LANES)),
            resident((nch, 1, 2 * LANES)),
            resident((nch, CONV_WIDTH * SUBLANES, LANES)),
            vec(), vec(), vec(),
            resident((d, d)),
            vec(),
        ],
        out_specs=pl.BlockSpec((tm, d), lambda i: (i, 0)),
        scratch_shapes=[pltpu.VMEM((tm, d), BF16),
                        pltpu.VMEM((nch, (HALO + seg) * SUBLANES, LANES), F32),
                        pltpu.VMEM((nch, seg * SUBLANES, LANES), F32),
                        pltpu.VMEM((tm, d), F32),
                        pltpu.VMEM((nch, HALO, LANES), F32)],
        compiler_params=_params(("arbitrary",), "mixer_a"),
        name="mixer_a",
    )(x, nrm, w_cat, b_cat, w_dw_b, b_dw, ln_g, ln_b, w_out, b_out)


def _mlp_kernel(h_ref, nrm_ref, wu_ref, wd_ref, fin_ref, o_ref, xn_ref, *, final, th):
    j = pl.program_id(1)

    def slices(first):
        for s in range(wu_ref.shape[1] // th):
            cs = slice(s * th, (s + 1) * th)
            t = jnp.dot(xn_ref[...], wu_ref[:, cs], preferred_element_type=F32)
            t = jnp.maximum(t, 0.0)
            t = (t * t).astype(BF16)
            part = jnp.dot(t, wd_ref[cs, :], preferred_element_type=F32)
            if first and s == 0:
                o_ref[...] = h_ref[...] + part
            else:
                o_ref[...] += part

    @pl.when(j == 0)
    def _():
        h = h_ref[...]
        xn_ref[...] = (h * _rms_scale(h) * nrm_ref[...]).astype(BF16)
        slices(True)

    @pl.when(j > 0)
    def _():
        slices(False)

    if final:
        @pl.when(j == pl.num_programs(1) - 1)
        def _():
            out = o_ref[...]
            o_ref[...] = out * _rms_scale(out) * fin_ref[...]


def _mlp(h, nrm, w_up, w_down, fin, *, layer, final, tm=512, tf=2048, th=1024):
    t, d = h.shape
    f = w_up.shape[2]
    return pl.pallas_call(
        functools.partial(_mlp_kernel, final=final, th=th),
        out_shape=jax.ShapeDtypeStruct((t, d), F32),
        grid=(t // tm, f // tf),
        in_specs=[
            pl.BlockSpec((tm, d), lambda i, j: (i, 0)),
            pl.BlockSpec((1, d), lambda i, j: (0, 0)),
            pl.BlockSpec((None, d, tf), lambda i, j: (layer, 0, j)),
            pl.BlockSpec((None, tf, d), lambda i, j: (layer, j, 0)),
            pl.BlockSpec((1, d), lambda i, j: (0, 0)),
        ],
        out_specs=pl.BlockSpec((tm, d), lambda i, j: (i, 0)),
        scratch_shapes=[pltpu.VMEM((tm, d), BF16)],
        compiler_params=_params(("parallel", "arbitrary"), "mlp"),
        name="mlp",
    )(h, nrm, w_up, w_down, fin)


def _qkv_kernel(h_ref, nq_ref, nkv_ref, w_ref, b_ref, o_ref, *, sub):
    nq = D_MODEL
    scale = 1.0 / math.sqrt(HEAD_DIM)
    for sb in range(h_ref.shape[0] // sub):
        rs = slice(sb * sub, (sb + 1) * sub)
        h = h_ref[rs, :]
        hs = h * _rms_scale(h)
        xq = (hs * nq_ref[...]).astype(BF16)
        xkv = (hs * nkv_ref[...]).astype(BF16)
        q = jnp.dot(xq, w_ref[:, :nq], preferred_element_type=F32) + b_ref[:, :nq]
        kv = jnp.dot(xkv, w_ref[:, nq:], preferred_element_type=F32) + b_ref[:, nq:]
        o_ref[rs, :nq] = (q * scale).astype(BF16)
        o_ref[rs, nq:] = kv.astype(BF16)


def _qkv(h, nq, nkv, w_qkv, b_qkv, *, tm=512, sub=256):
    t, d = h.shape
    n = w_qkv.shape[1]
    return pl.pallas_call(
        functools.partial(_qkv_kernel, sub=sub),
        out_shape=jax.ShapeDtypeStruct((t, n), BF16),
        grid=(t // tm,),
        in_specs=[
            pl.BlockSpec((tm, d), lambda i: (i, 0)),
            pl.BlockSpec((1, d), lambda i: (0, 0)),
            pl.BlockSpec((1, d), lambda i: (0, 0)),
            pl.BlockSpec((d, n), lambda i: (0, 0), pipeline_mode=pl.Buffered(1)),
            pl.BlockSpec((1, n), lambda i: (0, 0)),
        ],
        out_specs=pl.BlockSpec((tm, n), lambda i: (i, 0)),
        compiler_params=_params(("parallel",), "qkv_proj"),
        name="qkv_proj",
    )(h, nq, nkv, w_qkv, b_qkv)


def _attn_kernel(sink_ref, q_ref, kc_ref, kp_ref, vc_ref, vp_ref, h_ref, wo_ref, bo_ref, out_ref, o_scr):
    n = pl.program_id(1)
    lo = lax.broadcasted_iota(jnp.int32, (BLOCK, LANES), 1) < HEAD_DIM
    qi = lax.broadcasted_iota(jnp.int32, (BLOCK, BLOCK), 0)
    ki = lax.broadcasted_iota(jnp.int32, (BLOCK, BLOCK), 1)
    own = ki <= qi
    m_lo = jnp.where(lo, 1.0, 0.0).astype(BF16)
    m_hi = jnp.where(lo, 0.0, 1.0).astype(BF16)
    nt = (((1,), (1,)), ((), ()))

    for blk in range(q_ref.shape[1] // BLOCK):
        rows = slice(blk * BLOCK, (blk + 1) * BLOCK)
        before = slice((blk - 1) * BLOCK, blk * BLOCK)
        for p in range(N_PAIRS):
            cs = slice(p * LANES, (p + 1) * LANES)
            kc, vc = kc_ref[0, rows, cs], vc_ref[0, rows, cs]
            if blk == 0:
                kp, vp = kp_ref[0, :, cs], vp_ref[0, :, cs]
            else:
                kp, vp = kc_ref[0, before, cs], vc_ref[0, before, cs]
            kext = jnp.concatenate([kp * m_lo, kc * m_lo, kp * m_hi, kc * m_hi], axis=0)
            vext = jnp.concatenate([
                jnp.concatenate([vp * m_lo, m_lo], axis=1),
                jnp.concatenate([vc * m_lo, m_lo], axis=1),
                jnp.concatenate([vp * m_hi, m_hi], axis=1),
                jnp.concatenate([vc * m_hi, m_hi], axis=1)], axis=0)
            q = jnp.concatenate(
                [q_ref[0, rows, (p * GROUP + m) * LANES:(p * GROUP + m + 1) * LANES] for m in range(GROUP)], axis=0)
            s = lax.dot_general(q, kext, nt, preferred_element_type=F32)

            probs, sink_terms = [], []
            for m in range(GROUP):
                sm = s[m * BLOCK:(m + 1) * BLOCK]
                parts, maxes = [], []
                for hh in range(2):
                    s_prev = sm[:, (2 * hh) * BLOCK:(2 * hh + 1) * BLOCK]
                    if blk == 0:
                        s_prev = jnp.where(n > 0, s_prev, -jnp.inf)
                    s_own = sm[:, (2 * hh + 1) * BLOCK:(2 * hh + 2) * BLOCK]
                    sel = jnp.where(own, s_own, s_prev)
                    mx = jnp.maximum(jnp.max(sel, axis=-1, keepdims=True), sink_ref[(2 * p + hh) * GROUP + m])
                    e = jnp.exp(sel - mx)
                    parts += [jnp.where(own, 0.0, e), jnp.where(own, e, 0.0)]
                    maxes.append(mx)
                probs.append(jnp.concatenate(parts, axis=1).astype(BF16))
                sink_tile = jnp.where(lo, sink_ref[(2 * p) * GROUP + m], sink_ref[(2 * p + 1) * GROUP + m])
                sink_terms.append(jnp.exp(sink_tile - jnp.where(lo, maxes[0], maxes[1])))

            res = jnp.dot(jnp.concatenate(probs, axis=0), vext, preferred_element_type=F32)
            for m in range(GROUP):
                rm = res[m * BLOCK:(m + 1) * BLOCK]
                out = rm[:, :LANES] / (rm[:, LANES:] + sink_terms[m])
                o_scr[rows, (p * GROUP + m) * LANES:(p * GROUP + m + 1) * LANES] = out.astype(BF16)

    out_ref[0] = h_ref[0] + jnp.dot(o_scr[...], wo_ref[...], preferred_element_type=F32) + bo_ref[...]


def _attention(qkv3, sinks, h3, w_o, b_o, *, qb=4):
    b, s, _ = qkv3.shape
    d = D_MODEL
    rows = qb * BLOCK
    kblk = d // D_KV
    prev = lambda n: jnp.maximum(qb * n - 1, 0)
    return pl.pallas_call(
        _attn_kernel,
        out_shape=jax.ShapeDtypeStruct((b, s, d), F32),
        grid_spec=pltpu.PrefetchScalarGridSpec(
            num_scalar_prefetch=1,
            grid=(b, s // rows),
            in_specs=[
                pl.BlockSpec((1, rows, d), lambda bi, n, sk: (bi, n, 0)),
                pl.BlockSpec((1, rows, D_KV), lambda bi, n, sk: (bi, n, kblk)),
                pl.BlockSpec((1, BLOCK, D_KV), lambda bi, n, sk: (bi, prev(n), kblk)),
                pl.BlockSpec((1, rows, D_KV), lambda bi, n, sk: (bi, n, kblk + 1)),
                pl.BlockSpec((1, BLOCK, D_KV), lambda bi, n, sk: (bi, prev(n), kblk + 1)),
                pl.BlockSpec((1, rows, d), lambda bi, n, sk: (bi, n, 0)),
                pl.BlockSpec((d, d), lambda bi, n, sk: (0, 0), pipeline_mode=pl.Buffered(1)),
                pl.BlockSpec((1, d), lambda bi, n, sk: (0, 0)),
            ],
            out_specs=pl.BlockSpec((1, rows, d), lambda bi, n, sk: (bi, n, 0)),
            scratch_shapes=[pltpu.VMEM((rows, d), BF16)],
        ),
        compiler_params=_params(("parallel", "arbitrary"), "swa_attn"),
        name="swa_attn",
    )(sinks, qkv3, qkv3, qkv3, qkv3, qkv3, h3, w_o, b_o)


def _pair_heads(w):
    lead = w.shape[:-1]
    w = w.reshape(lead + (N_PAIRS, 2, GROUP, HEAD_DIM))
    return jnp.swapaxes(w, -3, -2).reshape(lead + (N_HEADS * HEAD_DIM,))


def kernel(x, a_norm, a_w_in, a_b_in, a_w_dw, a_b_dw, a_ln_g, a_ln_b, a_w_out, a_b_out, kv_norm, w_k, b_k, w_v, b_v, b_norm, b_w_q, b_b_q, b_sinks, b_w_o, b_b_o, mlp_norm, mlp_w_up, mlp_w_down, final_norm):
    bsz, seq, d = x.shape
    t = bsz * seq
    row = lambda v: v.reshape(1, -1)
    ones = jnp.ones((1, d), F32)
    n_a, n_b = a_norm.shape[0], b_norm.shape[0]
    assert n_b == 1, "shared K/V is projected together with the (single) mixer-B layer's queries"
    w_up, w_down = mlp_w_up.astype(BF16), mlp_w_down.astype(BF16)

    h = x.reshape(t, d)
    for i in range(n_a + n_b):
        if i < n_a:
            nch = d // LANES
            w_dw_b = jnp.broadcast_to(a_w_dw[i].reshape(CONV_WIDTH, nch, 1, LANES), (CONV_WIDTH, nch, SUBLANES, LANES))
            w_dw_b = w_dw_b.transpose(1, 0, 2, 3).reshape(nch, CONV_WIDTH * SUBLANES, LANES)
            w_cat = a_w_in[i].astype(BF16).reshape(d, 2, nch, LANES).transpose(2, 0, 1, 3).reshape(nch, d, 2 * LANES)
            b_cat = a_b_in[i].reshape(2, nch, LANES).transpose(1, 0, 2).reshape(nch, 1, 2 * LANES)
            h = _mixer_a(h, row(a_norm[i]), w_cat, b_cat, w_dw_b, row(a_b_dw[i]),
                         row(a_ln_g[i]), row(a_ln_b[i]), a_w_out[i].astype(BF16), row(a_b_out[i]), seq=seq)
        else:
            l = i - n_a
            if l == 0:
                w_kv = jnp.concatenate([w_k, w_v], axis=1)
                b_kv = jnp.concatenate([b_k, b_v])
            w_qkv = jnp.concatenate([_pair_heads(b_w_q[l].astype(BF16)), w_kv.astype(BF16)], axis=1)
            b_qkv = row(jnp.concatenate([_pair_heads(b_b_q[l]), b_kv]))
            qkv = _qkv(h, row(b_norm[l]), row(kv_norm), w_qkv, b_qkv)
            w_o = _pair_heads(b_w_o[l].astype(BF16).T).T
            h = _attention(qkv.reshape(bsz, seq, -1), b_sinks[l], h.reshape(bsz, seq, d), w_o,
                           row(b_b_o[l])).reshape(t, d)
        last = i == n_a + n_b - 1
        h = _mlp(h, row(mlp_norm[i]), w_up, w_down, row(final_norm) if last else ones, layer=i, final=last)
    return h.reshape(bsz, seq, d)
```
